```python
import math
import jax, jax.numpy as jnp
from jax import lax
import numpy as np

D_MODEL = 1024
BATCH = 8
SEQ = 2048
DEPTH = 4
DEC_BATCH = 128
DEC_SEQ = 8
PAST_LEN = 16384
PAGE_SIZE = 128

N_META = 16
EPS = 1e-6
GLA_HEADS = 4
GLA_DK = 64
GLA_DV = 128
GLA_KW = GLA_HEADS * GLA_DK
GLA_VW = GLA_HEADS * GLA_DV
GLA_RANK = 16
GLA_TAU = 16.0
GLA_CHUNK = 16
SSD_HEADS = 8
SSD_HEADDIM = 64
SSD_INNER = SSD_HEADS * SSD_HEADDIM
SSD_GROUPS = 2
SSD_REP = SSD_HEADS // SSD_GROUPS
SSD_DSTATE = 128
SSD_CONV = 4
SSD_CHUNK = 64
CONV_DIM = SSD_INNER + 2 * SSD_GROUPS * SSD_DSTATE
MIX_WIDTH = GLA_VW + SSD_INNER
PEER_HEADS = 8
PEER_NKEYS = 128
PEER_EXPERTS = PEER_NKEYS * PEER_NKEYS
PEER_QDIM = 128
PEER_TOPK = 16
PEER_BLOCK = 256
IN_SPLITS = (GLA_KW, GLA_KW, GLA_VW, GLA_VW, GLA_RANK, SSD_INNER, CONV_DIM, SSD_HEADS)
IN_WIDTH = GLA_KW + GLA_KW + GLA_VW + GLA_VW + GLA_RANK + SSD_INNER + CONV_DIM + SSD_HEADS

kernel_name = 'hymba_gla_ssd_peer_decoder_step'


def rmsnorm(x, w):
    xf = x.astype(jnp.float32)
    y = xf * lax.rsqrt(jnp.mean(xf * xf, axis=-1, keepdims=True) + EPS)
    return (y * w.astype(jnp.float32)).astype(x.dtype)


def segments(seq_len, split):
    return [(0, split), (split, seq_len)] if split > 0 else [(0, seq_len)]


def gla_segment(q, k, v, gk, s0):
    bsz, seg_len = q.shape[0], q.shape[1]
    c = math.gcd(seg_len, GLA_CHUNK)
    n = seg_len // c
    q, k, v, gk = (t.reshape((bsz, n, c) + t.shape[2:]) for t in (q, k, v, gk))
    b = jnp.cumsum(gk, axis=2)
    causal = jnp.tril(jnp.ones((c, c), dtype=bool))[None, None, :, :, None, None]
    decay = jnp.exp(jnp.where(causal, b[:, :, :, None] - b[:, :, None], -jnp.inf))
    attn = jnp.sum(q[:, :, :, None] * decay * k[:, :, None], axis=-1)
    o_intra = jnp.einsum('bntsh,bnshe->bnthe', attn, v)
    b_last = b[:, :, -1]
    s_local = jnp.einsum('bnshd,bnshe->bnhde', k * jnp.exp(b_last[:, :, None] - b), v)

    def step(s, inp):
        bl, sl = inp
        return jnp.exp(bl)[..., None] * s + sl, s

    s_fin, s_start = lax.scan(step, s0, (jnp.moveaxis(b_last, 1, 0), jnp.moveaxis(s_local, 1, 0)))
    s_start = jnp.moveaxis(s_start, 0, 1)
    o_inter = jnp.einsum('bnthd,bnhde->bnthe', q * jnp.exp(b), s_start)
    o = (o_intra + o_inter).reshape((bsz, seg_len) + o_intra.shape[3:])
    return o, s_fin


def ssd_segment(x, dt, a, bm, cm, h0):
    bsz, seg_len = x.shape[0], x.shape[1]
    c = math.gcd(seg_len, SSD_CHUNK)
    n = seg_len // c
    x, dt, bm, cm = (t.reshape((bsz, n, c) + t.shape[2:]) for t in (x, dt, bm, cm))
    cum = jnp.cumsum(dt * a, axis=2)
    causal = jnp.tril(jnp.ones((c, c), dtype=bool))[None, None, :, :, None, None]
    lmat = jnp.exp(jnp.where(causal, cum[:, :, :, None] - cum[:, :, None], -jnp.inf))
    cb = jnp.einsum('bntgk,bnsgk->bntsg', cm, bm)
    m = cb[..., None] * lmat * dt[:, :, None]
    y_intra = jnp.einsum('bntsgr,bnsgrp->bntgrp', m, x)
    cum_last = cum[:, :, -1]
    xw = x * (jnp.exp(cum_last[:, :, None] - cum) * dt)[..., None]
    s_local = jnp.einsum('bnsgk,bnsgrp->bngrpk', bm, xw)

    def step(h, inp):
        cl, sl = inp
        return jnp.exp(cl)[..., None, None] * h + sl, h

    h_fin, h_start = lax.scan(step, h0, (jnp.moveaxis(cum_last, 1, 0), jnp.moveaxis(s_local, 1, 0)))
    h_start = jnp.moveaxis(h_start, 0, 1)
    y_inter = jnp.einsum('bntgk,bngrpk->bntgrp', cm, h_start) * jnp.exp(cum)[..., None]
    y = (y_intra + y_inter).reshape((bsz, seg_len) + y_intra.shape[3:])
    return y, h_fin


def causal_conv(xbc, buf, w, b):
    seq_len = xbc.shape[1]
    full = jnp.concatenate([buf.astype(xbc.dtype), xbc], axis=1)
    out = b + full[:, 0:seq_len] * w[0]
    for i in range(1, SSD_CONV):
        out = out + full[:, i:i + seq_len] * w[i]
    return out, full[:, -(SSD_CONV - 1):]


def token_mixers(xn, s_gla, s_ssm, s_conv, split, w_in, w_gk2, b_gk2, gla_norm_w,
                 conv_w, conv_b, dt_bias, a_log, d_skip, ssd_norm_w, w_out):
    f32 = jnp.float32
    bsz, seq_len, _ = xn.shape
    dtype = xn.dtype
    offs = np.cumsum(IN_SPLITS)[:-1].tolist()
    q, k, v, g, f_lr, z, xbc, dt_raw = jnp.split(xn @ w_in, offs, axis=-1)
    segs = segments(seq_len, split)
    q = (q.astype(f32) * GLA_DK ** -0.5).reshape(bsz, seq_len, GLA_HEADS, GLA_DK)
    k = k.astype(f32).reshape(bsz, seq_len, GLA_HEADS, GLA_DK)
    v = v.astype(f32).reshape(bsz, seq_len, GLA_HEADS, GLA_DV)
    gk = (jax.nn.log_sigmoid((f_lr @ w_gk2 + b_gk2).astype(f32)) / GLA_TAU).reshape(bsz, seq_len, GLA_HEADS, GLA_DK)
    s = s_gla.astype(f32)
    outs = []
    for lo, hi in segs:
        o, s = gla_segment(q[:, lo:hi], k[:, lo:hi], v[:, lo:hi], gk[:, lo:hi], s)
        outs.append(o)
    o = jnp.concatenate(outs, axis=1)
    o_gla = (rmsnorm(o, gla_norm_w) * jax.nn.silu(g.astype(f32)).reshape(bsz, seq_len, GLA_HEADS, GLA_DV)).reshape(bsz, seq_len, GLA_VW)
    xbc_c, new_conv = causal_conv(xbc, s_conv, conv_w, conv_b)
    xbc_c = jax.nn.silu(xbc_c.astype(f32))
    xs, bm, cm = jnp.split(xbc_c, [SSD_INNER, SSD_INNER + SSD_GROUPS * SSD_DSTATE], axis=-1)
    xs = xs.reshape(bsz, seq_len, SSD_GROUPS, SSD_REP, SSD_HEADDIM)
    bm = bm.reshape(bsz, seq_len, SSD_GROUPS, SSD_DSTATE)
    cm = cm.reshape(bsz, seq_len, SSD_GROUPS, SSD_DSTATE)
    dt = jax.nn.softplus(dt_raw.astype(f32) + dt_bias.astype(f32)).reshape(bsz, seq_len, SSD_GROUPS, SSD_REP)
    a = -jnp.exp(a_log.astype(f32)).reshape(SSD_GROUPS, SSD_REP)
    h = s_ssm.astype(f32).reshape(bsz, SSD_GROUPS, SSD_REP, SSD_HEADDIM, SSD_DSTATE)
    ys = []
    for lo, hi in segs:
        y, h = ssd_segment(xs[:, lo:hi], dt[:, lo:hi], a, bm[:, lo:hi], cm[:, lo:hi], h)
        ys.append(y)
    y = jnp.concatenate(ys, axis=1) + d_skip.astype(f32).reshape(SSD_GROUPS, SSD_REP)[..., None] * xs
    y = rmsnorm(y.reshape(bsz, seq_len, SSD_INNER) * jax.nn.silu(z.astype(f32)), ssd_norm_w)
    mix = jnp.concatenate([o_gla, y], axis=-1).astype(dtype)
    out = mix @ w_out
    new_ssm = h.reshape(bsz, SSD_HEADS, SSD_HEADDIM, SSD_DSTATE)
    return out, s.astype(s_gla.dtype), new_ssm.astype(s_ssm.dtype), new_conv.astype(s_conv.dtype)


def peer(xn, w_q, sub_k1, sub_k2, u, v):
    shape = xn.shape
    t = xn.reshape(-1, D_MODEL)
    n_tok = t.shape[0]
    q = (t @ w_q).astype(jnp.float32).reshape(n_tok, PEER_HEADS, 2, PEER_QDIM // 2)
    s1 = jnp.einsum('thd,kd->thk', q[:, :, 0], sub_k1.astype(jnp.float32))
    s2 = jnp.einsum('thd,kd->thk', q[:, :, 1], sub_k2.astype(jnp.float32))
    v1, i1 = lax.top_k(s1, PEER_TOPK)
    v2, i2 = lax.top_k(s2, PEER_TOPK)
    cand = (v1[..., :, None] + v2[..., None, :]).reshape(n_tok, PEER_HEADS, PEER_TOPK * PEER_TOPK)
    sc, ci = lax.top_k(cand, PEER_TOPK)
    e = (jnp.take_along_axis(i1, ci // PEER_TOPK, axis=-1) * PEER_NKEYS
         + jnp.take_along_axis(i2, ci % PEER_TOPK, axis=-1))
    gate = jax.nn.softmax(sc, axis=-1)
    e = e.reshape(n_tok, PEER_HEADS * PEER_TOPK)
    gate = gate.reshape(n_tok, PEER_HEADS * PEER_TOPK).astype(xn.dtype)
    pad = (-n_tok) % PEER_BLOCK
    tp = jnp.pad(t, ((0, pad), (0, 0))).reshape(-1, PEER_BLOCK, D_MODEL)
    ep = jnp.pad(e, ((0, pad), (0, 0))).reshape(-1, PEER_BLOCK, PEER_HEADS * PEER_TOPK)
    gp = jnp.pad(gate, ((0, pad), (0, 0))).reshape(-1, PEER_BLOCK, PEER_HEADS * PEER_TOPK)

    def blk(args):
        tb, eb, gb = args
        hid = jnp.einsum('tkd,td->tk', u[eb], tb)
        return jnp.einsum('tk,tkd->td', gb * jax.nn.gelu(hid), v[eb])

    out = lax.map(blk, (tp, ep, gp))
    return out.reshape(-1, D_MODEL)[:n_tok].reshape(shape)


def setup_inputs(seed: int = 0) -> dict:
    key = jax.random.key(seed)
    ks = jax.random.split(key, 32)
    f32 = jnp.float32

    def nrm(i, shape, s):
        return jax.random.normal(ks[i], shape, f32) * s

    dt0 = jnp.exp(jax.random.uniform(ks[10], (DEPTH, SSD_HEADS), f32, math.log(1e-3), math.log(1e-1)))
    return {
        'x_prompt': nrm(0, (BATCH, SEQ, D_MODEL), 1.0),
        'x_sample': nrm(1, (DEC_BATCH, DEC_SEQ, D_MODEL), 1.0),
        'state_gla': nrm(2, (DEPTH, DEC_BATCH, GLA_HEADS, GLA_DK, GLA_DV), 0.5),
        'state_ssm': nrm(3, (DEPTH, DEC_BATCH, SSD_HEADS, SSD_HEADDIM, SSD_DSTATE), 0.5),
        'state_conv': nrm(4, (DEPTH, DEC_BATCH, SSD_CONV - 1, CONV_DIM), 1.0),
        'meta_tokens': nrm(5, (N_META, D_MODEL), 1.0),
        'norm1_w': 1.0 + nrm(6, (DEPTH, D_MODEL), 0.02),
        'w_in': nrm(7, (DEPTH, D_MODEL, IN_WIDTH), D_MODEL ** -0.5),
        'w_gk2': nrm(8, (DEPTH, GLA_RANK, GLA_KW), GLA_RANK ** -0.5),
        'b_gk2': nrm(9, (DEPTH, GLA_KW), 0.01) + 1.0,
        'gla_norm_w': 1.0 + nrm(11, (DEPTH, GLA_DV), 0.02),
        'conv_w': nrm(12, (DEPTH, SSD_CONV, CONV_DIM), 0.5),
        'conv_b': nrm(13, (DEPTH, CONV_DIM), 0.01),
        'dt_bias': dt0 + jnp.log(-jnp.expm1(-dt0)),
        'a_log': jnp.log(jax.random.uniform(ks[14], (DEPTH, SSD_HEADS), f32, 1.0, 16.0)),
        'd_skip': 1.0 + nrm(15, (DEPTH, SSD_HEADS), 0.01),
        'ssd_norm_w': 1.0 + nrm(16, (DEPTH, SSD_INNER), 0.02),
        'w_out': nrm(17, (DEPTH, MIX_WIDTH, D_MODEL), MIX_WIDTH ** -0.5),
        'norm2_w': 1.0 + nrm(18, (DEPTH, D_MODEL), 0.02),
        'peer_wq': nrm(19, (DEPTH, D_MODEL, PEER_HEADS * PEER_QDIM), D_MODEL ** -0.5),
        'peer_k1': nrm(20, (DEPTH, PEER_NKEYS, PEER_QDIM // 2), (PEER_QDIM // 2) ** -0.5),
        'peer_k2': nrm(21, (DEPTH, PEER_NKEYS, PEER_QDIM // 2), (PEER_QDIM // 2) ** -0.5),
        'peer_u': nrm(22, (DEPTH, PEER_EXPERTS, D_MODEL), D_MODEL ** -0.5),
        'peer_v': nrm(23, (DEPTH, PEER_EXPERTS, D_MODEL), (PEER_HEADS * PEER_TOPK) ** -0.5),
        'final_norm_w': 1.0 + nrm(24, (D_MODEL,), 0.02),
    }


def reference(x_prompt, x_sample, state_gla, state_ssm, state_conv, meta_tokens, norm1_w, w_in,
              w_gk2, b_gk2, gla_norm_w, conv_w, conv_b, dt_bias, a_log, d_skip, ssd_norm_w, w_out,
              norm2_w, peer_wq, peer_k1, peer_k2, peer_u, peer_v, final_norm_w):
    dtype = x_prompt.dtype
    bp = x_prompt.shape[0]
    hp = jnp.concatenate([jnp.broadcast_to(meta_tokens[None].astype(dtype), (bp, N_META, D_MODEL)), x_prompt], axis=1)
    hs = x_sample
    gp = jnp.zeros((bp, GLA_HEADS, GLA_DK, GLA_DV), dtype)
    sp = jnp.zeros((bp, SSD_HEADS, SSD_HEADDIM, SSD_DSTATE), dtype)
    cp = jnp.zeros((bp, SSD_CONV - 1, CONV_DIM), dtype)

    def layer(h, sg, ss, sc, split, l):
        m, sg, ss, sc = token_mixers(rmsnorm(h, norm1_w[l]), sg, ss, sc, split, w_in[l], w_gk2[l], b_gk2[l],
                                     gla_norm_w[l], conv_w[l], conv_b[l], dt_bias[l], a_log[l], d_skip[l],
                                     ssd_norm_w[l], w_out[l])
        h = h + m
        h = h + peer(rmsnorm(h, norm2_w[l]), peer_wq[l], peer_k1[l], peer_k2[l], peer_u[l], peer_v[l])
        return h, sg, ss, sc

    gla_p, ssm_p, conv_p, gla_s, ssm_s, conv_s = [], [], [], [], [], []
    for l in range(DEPTH):
        hp, g1, s1, c1 = layer(hp, gp, sp, cp, N_META, l)
        hs, g2, s2, c2 = layer(hs, state_gla[l], state_ssm[l], state_conv[l], 0, l)
        gla_p.append(g1); ssm_p.append(s1); conv_p.append(c1)
        gla_s.append(g2); ssm_s.append(s2); conv_s.append(c2)
    y_prompt = rmsnorm(hp, final_norm_w)[:, N_META:]
    y_sample = rmsnorm(hs, final_norm_w)
    return (y_prompt, y_sample, jnp.stack(gla_p), jnp.stack(ssm_p), jnp.stack(conv_p),
            jnp.stack(gla_s), jnp.stack(ssm_s), jnp.stack(conv_s))
```

```python
import functools
import math

import jax
import jax.numpy as jnp
from jax import lax
from jax.experimental import pallas as pl
from jax.experimental.pallas import tpu as pltpu

F32 = jnp.float32
BF16 = jnp.bfloat16

D_MODEL = 1024
N_META = 16
EPS = 1e-6
GLA_HEADS = 4
GLA_DK = 64
GLA_DV = 128
GLA_KW = GLA_HEADS * GLA_DK
GLA_VW = GLA_HEADS * GLA_DV
GLA_RANK = 16
GLA_TAU = 16.0
SSD_HEADS = 8
SSD_HEADDIM = 64
SSD_INNER = SSD_HEADS * SSD_HEADDIM
SSD_GROUPS = 2
SSD_REP = SSD_HEADS // SSD_GROUPS
SSD_DSTATE = 128
SSD_CONV = 4
CONV_DIM = SSD_INNER + 2 * SSD_GROUPS * SSD_DSTATE
PEER_HEADS = 8
PEER_NKEYS = 128
PEER_QDIM = 128
PEER_TOPK = 16

LANES = 128
SUBLANES = 8
MISC_W = LANES
QKVG_W = 2 * GLA_KW + 2 * GLA_VW
ZX_W = SSD_INNER + CONV_DIM
PROJ_W = QKVG_W + ZX_W + MISC_W
PROMPT_ROWS = 48
GLA_CHUNK = 16
TOKEN_TILE = 512
ROUTE_TILE = 256
PEER_A_PER_STEP = 4
VMEM_LIMIT = 56 * 1024 * 1024

NT_DIMS = (((1,), (1,)), ((), ()))
TN_DIMS = (((0,), (0,)), ((), ()))


def _rms(x, w):
    return x * lax.rsqrt(jnp.mean(x * x, axis=-1, keepdims=True) + EPS) * w


def _softplus(x):
    return jnp.maximum(x, 0.0) + jnp.log1p(jnp.exp(-jnp.abs(x)))


def _silu(x):
    return x * (1.0 / (1.0 + jnp.exp(-x)))


def _gelu_tanh(x):
    c = math.sqrt(2.0 / math.pi)
    return 0.5 * x * (1.0 + jnp.tanh(c * (x + 0.044715 * (x * x * x))))


def _chunk_cumsum(x, ch):
    row = lax.broadcasted_iota(jnp.int32, x.shape, 0) % ch
    sh = 1
    while sh < ch:
        x = x + jnp.where(row >= sh, pltpu.roll(x, sh, 0), 0.0)
        sh *= 2
    return x


def _in_proj_kernel(h_ref, nw_ref, w_ref, wgk_ref, bgk_ref, qkvg_ref, gk_ref, zx_ref, dt_ref):
    xn = _rms(h_ref[...], nw_ref[...]).astype(BF16)
    y = jnp.dot(xn, w_ref[...], preferred_element_type=F32)
    qkvg_ref[:, 0:GLA_KW] = y[:, 0:GLA_KW] * (GLA_DK ** -0.5)
    qkvg_ref[:, GLA_KW:QKVG_W] = y[:, GLA_KW:QKVG_W]
    zx_ref[...] = y[:, QKVG_W:QKVG_W + ZX_W]
    misc = y[:, QKVG_W + ZX_W:PROJ_W]
    dt_ref[...] = misc
    pre = jnp.dot(misc.astype(BF16), wgk_ref[...], preferred_element_type=F32) + bgk_ref[...]
    log_sig = jnp.minimum(pre, 0.0) - jnp.log1p(jnp.exp(-jnp.abs(pre)))
    gk_ref[...] = log_sig / GLA_TAU


def _in_proj(h, nw, w, wgk, bgk, tm):
    t_pad = h.shape[0]
    row = lambda i: (i, 0)
    fixed = lambda i: (0, 0)
    return pl.pallas_call(
        _in_proj_kernel,
        grid=(t_pad // tm,),
        in_specs=[
            pl.BlockSpec((tm, D_MODEL), row),
            pl.BlockSpec((1, D_MODEL), fixed),
            pl.BlockSpec((D_MODEL, PROJ_W), fixed),
            pl.BlockSpec((MISC_W, GLA_KW), fixed),
            pl.BlockSpec((1, GLA_KW), fixed),
        ],
        out_specs=[
            pl.BlockSpec((tm, QKVG_W), row),
            pl.BlockSpec((tm, GLA_KW), row),
            pl.BlockSpec((tm, ZX_W), row),
            pl.BlockSpec((tm, MISC_W), row),
        ],
        out_shape=[
            jax.ShapeDtypeStruct((t_pad, QKVG_W), F32),
            jax.ShapeDtypeStruct((t_pad, GLA_KW), F32),
            jax.ShapeDtypeStruct((t_pad, ZX_W), F32),
            jax.ShapeDtypeStruct((t_pad, MISC_W), F32),
        ],
        compiler_params=pltpu.CompilerParams(
            dimension_semantics=("arbitrary",), vmem_limit_bytes=VMEM_LIMIT),
        name="in_proj",
    )(h, nw, w, wgk, bgk)


def _gla_kernel(qkvg_ref, gk_ref, s0_ref, nw_ref, o_ref, st_ref, *, nsub, ch):
    @pl.when(pl.program_id(1) == 0)
    def _():
        st_ref[...] = s0_ref[...]

    rb = nsub * ch
    q = qkvg_ref[:, 0:GLA_KW]
    k = qkvg_ref[:, GLA_KW:2 * GLA_KW]
    v = qkvg_ref[:, 2 * GLA_KW:2 * GLA_KW + GLA_VW]
    g = qkvg_ref[:, 2 * GLA_KW + GLA_VW:QKVG_W]
    b = _chunk_cumsum(gk_ref[...], ch)

    b3 = b.reshape(nsub, ch, GLA_KW)
    q3 = q.reshape(nsub, ch, GLA_KW)
    k3 = k.reshape(nsub, ch, GLA_KW)
    v3 = v.reshape(nsub, ch, GLA_VW)
    tix = lax.broadcasted_iota(jnp.int32, (nsub, ch, GLA_KW), 1)
    lane = lax.broadcasted_iota(jnp.int32, (nsub, ch, LANES), 2)
    low_half = lane < GLA_DK
    acc = [jnp.zeros((nsub, ch, GLA_DV), F32) for _ in range(GLA_HEADS)]
    for s in range(ch):
        diff = b3 - b3[:, s:s + 1, :]
        e = jnp.exp(jnp.where(tix >= s, diff, -jnp.inf))
        f = q3 * e * k3[:, s:s + 1, :]
        for h in range(GLA_HEADS):
            fh = f[:, :, (h // 2) * LANES:(h // 2 + 1) * LANES]
            mask = low_half if h % 2 == 0 else jnp.logical_not(low_half)
            col = jnp.sum(jnp.where(mask, fh, 0.0), axis=-1, keepdims=True)
            acc[h] = acc[h] + col * v3[:, s:s + 1, h * GLA_DV:(h + 1) * GLA_DV]

    nw = nw_ref[...]
    for i in range(nsub):
        r0 = i * ch
        bi = b[r0:r0 + ch]
        bl = bi[ch - 1:ch]
        qh = (q[r0:r0 + ch] * jnp.exp(bi)).astype(BF16)
        kh = (k[r0:r0 + ch] * jnp.exp(bl - bi)).astype(BF16)
        vi = v[r0:r0 + ch].astype(BF16)
        st = st_ref[0]
        stb = st.astype(BF16)
        kv = []
        for h in range(GLA_HEADS):
            ks = slice(h * GLA_DK, (h + 1) * GLA_DK)
            vs = slice(h * GLA_DV, (h + 1) * GLA_DV)
            o_inter = lax.dot_general(qh[:, ks], stb[:, ks], NT_DIMS, preferred_element_type=F32)
            kv.append(lax.dot_general(vi[:, vs], kh[:, ks], TN_DIMS, preferred_element_type=F32))
            o = acc[h][i] + o_inter
            gate = _silu(g[r0:r0 + ch, vs])
            o_ref[r0:r0 + ch, vs] = _rms(o, nw) * gate
        st_ref[0] = st * jnp.exp(bl) + jnp.concatenate(kv, axis=1)


def _gla(qkvg, gk, s0t, nw, *, row0, nseq, nblk, nsub, ch):
    rb = nsub * ch
    blk0 = row0 // rb
    row = lambda b, j: (blk0 + b * nblk + j, 0)
    seq = lambda b, j: (b, 0, 0)
    return pl.pallas_call(
        functools.partial(_gla_kernel, nsub=nsub, ch=ch),
        grid=(nseq, nblk),
        in_specs=[
            pl.BlockSpec((rb, QKVG_W), row),
            pl.BlockSpec((rb, GLA_KW), row),
            pl.BlockSpec((1, GLA_DV, GLA_KW), seq),
            pl.BlockSpec((1, GLA_DV), lambda b, j: (0, 0)),
        ],
        out_specs=[
            pl.BlockSpec((rb, GLA_VW), lambda b, j: (b * nblk + j, 0)),
            pl.BlockSpec((1, GLA_DV, GLA_KW), seq),
        ],
        out_shape=[
            jax.ShapeDtypeStruct((nseq * nblk * rb, GLA_VW), F32),
            jax.ShapeDtypeStruct((nseq, GLA_DV, GLA_KW), F32),
        ],
        compiler_params=pltpu.CompilerParams(
            dimension_semantics=("arbitrary", "arbitrary"), vmem_limit_bytes=VMEM_LIMIT),
        name="gla_scan",
    )(qkvg, gk, s0t, nw)


def _ssd_kernel(zx_ref, dt_ref, c0_ref, h0_ref, cw_ref, cb_ref, dtb_ref, alog_ref, dsk_ref, nw_ref,
                y_ref, hst_ref, cst_ref, carry_ref, *, ch):
    @pl.when(pl.program_id(1) == 0)
    def _():
        hst_ref[...] = h0_ref[...]
        carry_ref[...] = jnp.zeros_like(carry_ref)
        carry_ref[SUBLANES - (SSD_CONV - 1):SUBLANES, :] = c0_ref[0]

    z = zx_ref[:, 0:SSD_INNER]
    x = zx_ref[:, SSD_INNER:ZX_W]
    full = jnp.concatenate([carry_ref[...], x], axis=0)
    cw = cw_ref[...]
    base = SUBLANES - (SSD_CONV - 1)
    conv = cb_ref[...] + full[base:base + ch] * cw[0:1]
    for i in range(1, SSD_CONV):
        conv = conv + full[base + i:base + i + ch] * cw[i:i + 1]
    carry_ref[...] = full[ch:ch + SUBLANES]
    cst_ref[0] = full[ch + base:ch + SUBLANES]

    xc = _silu(conv)
    xs = xc[:, 0:SSD_INNER]
    bm = xc[:, SSD_INNER:SSD_INNER + SSD_GROUPS * SSD_DSTATE].astype(BF16)
    cm = xc[:, SSD_INNER + SSD_GROUPS * SSD_DSTATE:CONV_DIM].astype(BF16)

    dt = _softplus(dt_ref[...] + dtb_ref[...])
    a = -jnp.exp(alog_ref[...])
    cum = _chunk_cumsum(dt * a, ch)
    eye = (lax.broadcasted_iota(jnp.int32, (SUBLANES, LANES), 0)
           == lax.broadcasted_iota(jnp.int32, (SUBLANES, LANES), 1)).astype(F32)
    cum_t = lax.dot_general(eye, cum, NT_DIMS, precision=lax.Precision.HIGHEST,
                            preferred_element_type=F32)
    dt_t = lax.dot_general(eye, dt, NT_DIMS, precision=lax.Precision.HIGHEST,
                           preferred_element_type=F32)
    causal = (lax.broadcasted_iota(jnp.int32, (ch, ch), 0)
              >= lax.broadcasted_iota(jnp.int32, (ch, ch), 1))
    dsk = dsk_ref[...]
    ys = []
    for grp in range(SSD_GROUPS):
        bg = bm[:, grp * SSD_DSTATE:(grp + 1) * SSD_DSTATE]
        cg = cm[:, grp * SSD_DSTATE:(grp + 1) * SSD_DSTATE]
        cb = lax.dot_general(cg, bg, NT_DIMS, preferred_element_type=F32)
        for rep in range(SSD_REP):
            hh = grp * SSD_REP + rep
            cum_c = cum[:, hh:hh + 1]
            lmat = jnp.exp(jnp.where(causal, cum_c - cum_t[hh:hh + 1, :], -jnp.inf))
            m = cb * lmat * dt_t[hh:hh + 1, :]
            xh = xs[:, hh * SSD_HEADDIM:(hh + 1) * SSD_HEADDIM]
            y_intra = jnp.dot(m.astype(BF16), xh.astype(BF16), preferred_element_type=F32)
            cl = cum[ch - 1:ch, hh:hh + 1]
            xw = xh * (jnp.exp(cl - cum_c) * dt[:, hh:hh + 1])
            hs = hst_ref[0, hh]
            y_inter = lax.dot_general(cg, hs.astype(BF16), NT_DIMS, preferred_element_type=F32)
            hst_ref[0, hh] = jnp.exp(cl) * hs + lax.dot_general(
                xw.astype(BF16), bg, TN_DIMS, preferred_element_type=F32)
            ys.append(y_intra + y_inter * jnp.exp(cum_c) + dsk[:, hh:hh + 1] * xh)
    y = jnp.concatenate(ys, axis=1) * _silu(z)
    y_ref[...] = _rms(y, nw_ref[...])


def _ssd(zx, dtr, c0, h0, cw, cb, dtb, alog, dsk, nw, *, row0, nseq, nblk, ch):
    blk0 = row0 // ch
    row = lambda b, j: (blk0 + b * nblk + j, 0)
    fixed = lambda b, j: (0, 0)
    return pl.pallas_call(
        functools.partial(_ssd_kernel, ch=ch),
        grid=(nseq, nblk),
        in_specs=[
            pl.BlockSpec((ch, ZX_W), row),
            pl.BlockSpec((ch, MISC_W), row),
            pl.BlockSpec((1, SSD_CONV - 1, CONV_DIM), lambda b, j: (b, 0, 0)),
            pl.BlockSpec((1, SSD_HEADS, SSD_HEADDIM, SSD_DSTATE), lambda b, j: (b, 0, 0, 0)),
            pl.BlockSpec((SSD_CONV, CONV_DIM), fixed),
            pl.BlockSpec((1, CONV_DIM), fixed),
            pl.BlockSpec((1, MISC_W), fixed),
            pl.BlockSpec((1, MISC_W), fixed),
            pl.BlockSpec((1, MISC_W), fixed),
            pl.BlockSpec((1, SSD_INNER), fixed),
        ],
        out_specs=[
            pl.BlockSpec((ch, SSD_INNER), lambda b, j: (b * nblk + j, 0)),
            pl.BlockSpec((1, SSD_HEADS, SSD_HEADDIM, SSD_DSTATE), lambda b, j: (b, 0, 0, 0)),
            pl.BlockSpec((1, SSD_CONV - 1, CONV_DIM), lambda b, j: (b, 0, 0)),
        ],
        out_shape=[
            jax.ShapeDtypeStruct((nseq * nblk * ch, SSD_INNER), F32),
            jax.ShapeDtypeStruct((nseq, SSD_HEADS, SSD_HEADDIM, SSD_DSTATE), F32),
            jax.ShapeDtypeStruct((nseq, SSD_CONV - 1, CONV_DIM), F32),
        ],
        scratch_shapes=[pltpu.VMEM((SUBLANES, CONV_DIM), F32)],
        compiler_params=pltpu.CompilerParams(
            dimension_semantics=("arbitrary", "arbitrary"), vmem_limit_bytes=VMEM_LIMIT),
        name="ssd_scan",
    )(zx, dtr, c0, h0, cw, cb, dtb, alog, dsk, nw)


def _out_proj_kernel(h_ref, og_ref, ys_ref, w_ref, o_ref):
    acc = jnp.dot(og_ref[...].astype(BF16), w_ref[0:GLA_VW, :], preferred_element_type=F32)
    acc = acc + jnp.dot(ys_ref[...].astype(BF16), w_ref[GLA_VW:GLA_VW + SSD_INNER, :],
                        preferred_element_type=F32)
    o_ref[...] = h_ref[...] + acc


def _out_proj(h, og, ys, w, tm):
    t_pad = h.shape[0]
    row = lambda i: (i, 0)
    return pl.pallas_call(
        _out_proj_kernel,
        grid=(t_pad // tm,),
        in_specs=[
            pl.BlockSpec((tm, D_MODEL), row),
            pl.BlockSpec((tm, GLA_VW), row),
            pl.BlockSpec((tm, SSD_INNER), row),
            pl.BlockSpec((GLA_VW + SSD_INNER, D_MODEL), lambda i: (0, 0)),
        ],
        out_specs=pl.BlockSpec((tm, D_MODEL), row),
        out_shape=jax.ShapeDtypeStruct((t_pad, D_MODEL), F32),
        compiler_params=pltpu.CompilerParams(
            dimension_semantics=("arbitrary",), vmem_limit_bytes=VMEM_LIMIT),
        name="out_proj",
    )(h, og, ys, w)


N_EXTRACT = PEER_TOPK + 1
VAL_ROWS = 24


def _top_values(s, tm):
    rowi = lax.broadcasted_iota(jnp.int32, (VAL_ROWS, tm), 0)
    vals = jnp.full((VAL_ROWS, tm), -jnp.inf, F32)
    for i in range(N_EXTRACT):
        m = jnp.max(s, axis=0, keepdims=True)
        vals = jnp.where(rowi == i, m, vals)
        s = jnp.where(s == m, -jnp.inf, s)
    return vals


def _route_kernel(h_ref, nw_ref, wqt_ref, kkt_ref, xnt_ref, s2_ref, p2_ref, c_ref, p1_ref, *, tm):
    xn = _rms(h_ref[...], nw_ref[...])
    xnb = xn.astype(BF16)
    xnt_ref[...] = xn.T.astype(BF16)
    qt = lax.dot_general(wqt_ref[...], xnb, NT_DIMS, preferred_element_type=F32)
    st = jnp.dot(kkt_ref[...], qt.astype(BF16), preferred_element_type=F32)
    row8 = lax.broadcasted_iota(jnp.int32, (SUBLANES, tm), 0)
    row24 = lax.broadcasted_iota(jnp.int32, (VAL_ROWS, tm), 0)
    for h in range(PEER_HEADS):
        r1 = h * 2 * PEER_NKEYS
        s1 = st[r1:r1 + PEER_NKEYS]
        s2 = st[r1 + PEER_NKEYS:r1 + 2 * PEER_NKEYS]
        v1 = _top_values(s1, tm)
        v2 = _top_values(s2, tm)
        pieces = [jnp.where(row24 < N_EXTRACT, v1[0:1] + v2, -jnp.inf)]
        for i in range(1, SUBLANES):
            lim = N_EXTRACT // (i + 1)
            pieces.append(jnp.where(row8 < lim, v1[i:i + 1] + v2[0:SUBLANES], -jnp.inf))
        pieces.append(v1[SUBLANES:VAL_ROWS] + v2[0:1])
        cand = jnp.concatenate(pieces, axis=0)
        best = _top_values(cand, tm)
        thr = 0.5 * (best[PEER_TOPK - 1:PEER_TOPK] + best[PEER_TOPK:PEER_TOPK + 1])
        top = best[0:1]
        zsum = jnp.sum(jnp.where(cand >= thr, jnp.exp(cand - top), 0.0), axis=0, keepdims=True)
        rows = slice(h * PEER_NKEYS, (h + 1) * PEER_NKEYS)
        s2_ref[rows, :] = s2
        p2_ref[rows, :] = jnp.exp(s2 - v2[0:1]) / zsum
        c_ref[rows, :] = thr - s1
        p1_ref[rows, :] = jnp.exp(s1 - v1[0:1])


def _route(h, nw, wqt, kkt, tm):
    t_pad = h.shape[0]
    hk = PEER_HEADS * PEER_NKEYS
    col = lambda i: (0, i)
    fixed = lambda i: (0, 0)
    return pl.pallas_call(
        functools.partial(_route_kernel, tm=tm),
        grid=(t_pad // tm,),
        in_specs=[
            pl.BlockSpec((tm, D_MODEL), lambda i: (i, 0)),
            pl.BlockSpec((1, D_MODEL), fixed),
            pl.BlockSpec((PEER_HEADS * PEER_QDIM, D_MODEL), fixed),
            pl.BlockSpec((2 * hk, PEER_HEADS * PEER_QDIM), fixed),
        ],
        out_specs=[pl.BlockSpec((D_MODEL, tm), col)] + [pl.BlockSpec((hk, tm), col)] * 4,
        out_shape=[jax.ShapeDtypeStruct((D_MODEL, t_pad), BF16)]
        + [jax.ShapeDtypeStruct((hk, t_pad), F32)] * 4,
        compiler_params=pltpu.CompilerParams(
            dimension_semantics=("arbitrary",), vmem_limit_bytes=VMEM_LIMIT),
        name="peer_route",
    )(h, nw, wqt, kkt)


def _peer_kernel(xnt_ref, u_ref, vt_ref, s2_ref, p2_ref, c_ref, p1_ref, h_ref, o_ref, acc_ref, *, na):
    e = pl.program_id(1)

    @pl.when(e == 0)
    def _():
        acc_ref[...] = jnp.zeros_like(acc_ref)

    hid = jnp.dot(u_ref[...], xnt_ref[...], preferred_element_type=F32)
    act = _gelu_tanh(hid)
    parts = []
    for al in range(na):
        a_idx = e * na + al
        gate = None
        for h in range(PEER_HEADS):
            rows = slice(h * PEER_NKEYS, (h + 1) * PEER_NKEYS)
            c_row = c_ref[pl.ds(h * PEER_NKEYS + a_idx, 1), :]
            p_row = p1_ref[pl.ds(h * PEER_NKEYS + a_idx, 1), :]
            term = jnp.where(s2_ref[rows, :] >= c_row, p2_ref[rows, :], 0.0) * p_row
            gate = term if gate is None else gate + term
        parts.append((act[al * PEER_NKEYS:(al + 1) * PEER_NKEYS] * gate).astype(BF16))
    weighted = jnp.concatenate(parts, axis=0)
    acc_ref[...] += jnp.dot(vt_ref[...], weighted, preferred_element_type=F32)

    @pl.when(e == pl.num_programs(1) - 1)
    def _():
        o_ref[...] = h_ref[...] + acc_ref[...].T


def _peer(xnt, u, vt, s2, p2, c, p1, h, tm, na):
    t_pad = h.shape[0]
    hk = PEER_HEADS * PEER_NKEYS
    te = na * PEER_NKEYS
    tok = lambda i, e: (0, i)
    return pl.pallas_call(
        functools.partial(_peer_kernel, na=na),
        grid=(t_pad // tm, PEER_NKEYS // na),
        in_specs=[
            pl.BlockSpec((D_MODEL, tm), tok),
            pl.BlockSpec((te, D_MODEL), lambda i, e: (e, 0)),
            pl.BlockSpec((D_MODEL, te), lambda i, e: (0, e)),
            pl.BlockSpec((hk, tm), tok),
            pl.BlockSpec((hk, tm), tok),
            pl.BlockSpec((hk, tm), tok),
            pl.BlockSpec((hk, tm), tok),
            pl.BlockSpec((tm, D_MODEL), lambda i, e: (i, 0)),
        ],
        out_specs=pl.BlockSpec((tm, D_MODEL), lambda i, e: (i, 0)),
        out_shape=jax.ShapeDtypeStruct((t_pad, D_MODEL), F32),
        scratch_shapes=[pltpu.VMEM((D_MODEL, tm), F32)],
        compiler_params=pltpu.CompilerParams(
            dimension_semantics=("arbitrary", "arbitrary"), vmem_limit_bytes=VMEM_LIMIT),
        name="peer_experts",
    )(xnt, u, vt, s2, p2, c, p1, h)


def _final_norm_kernel(h_ref, w_ref, o_ref):
    o_ref[...] = _rms(h_ref[...], w_ref[...])


def _final_norm(h, w, tm):
    t_pad = h.shape[0]
    return pl.pallas_call(
        _final_norm_kernel,
        grid=(t_pad // tm,),
        in_specs=[pl.BlockSpec((tm, D_MODEL), lambda i: (i, 0)),
                  pl.BlockSpec((1, D_MODEL), lambda i: (0, 0))],
        out_specs=pl.BlockSpec((tm, D_MODEL), lambda i: (i, 0)),
        out_shape=jax.ShapeDtypeStruct((t_pad, D_MODEL), F32),
        name="final_norm",
    )(h, w)


def _pad_lanes(x, width):
    return jnp.pad(x, ((0, 0), (0, width - x.shape[-1])))


def kernel(x_prompt, x_sample, state_gla, state_ssm, state_conv, meta_tokens, norm1_w, w_in, w_gk2, b_gk2, gla_norm_w, conv_w, conv_b, dt_bias, a_log, d_skip, ssd_norm_w, w_out, norm2_w, peer_wq, peer_k1, peer_k2, peer_u, peer_v, final_norm_w):
    bp, seq_p, _ = x_prompt.shape
    bs, seq_s, _ = x_sample.shape
    depth = w_in.shape[0]
    lp = seq_p + N_META
    tp, ts = bp * lp, bs * seq_s
    tm = TOKEN_TILE
    t_pad = -(-(tp + ts) // tm) * tm
    assert lp % PROMPT_ROWS == 0 and seq_s % SUBLANES == 0 and tp % seq_s == 0
    assert SSD_CONV - 1 <= seq_s and t_pad % ROUTE_TILE == 0

    hp = jnp.concatenate([jnp.broadcast_to(meta_tokens[None], (bp, N_META, D_MODEL)), x_prompt], axis=1)
    h = jnp.concatenate([hp.reshape(tp, D_MODEL), x_sample.reshape(ts, D_MODEL),
                         jnp.zeros((t_pad - tp - ts, D_MODEL), F32)], axis=0)

    offs = [0, GLA_KW, 2 * GLA_KW, 2 * GLA_KW + GLA_VW, QKVG_W, QKVG_W + GLA_RANK,
            QKVG_W + GLA_RANK + SSD_INNER, QKVG_W + GLA_RANK + SSD_INNER + CONV_DIM]
    w_flr = w_in[:, :, offs[4]:offs[5]]
    w_zx = w_in[:, :, offs[5]:offs[7]]
    w_dt = w_in[:, :, offs[7]:offs[7] + SSD_HEADS]
    w_misc = jnp.concatenate(
        [w_dt, w_flr, jnp.zeros((depth, D_MODEL, MISC_W - SSD_HEADS - GLA_RANK), F32)], axis=-1)
    w_proj = jnp.concatenate([w_in[:, :, 0:QKVG_W], w_zx, w_misc], axis=-1).astype(BF16)
    w_gk = jnp.concatenate(
        [jnp.zeros((depth, SSD_HEADS, GLA_KW), F32), w_gk2,
         jnp.zeros((depth, MISC_W - SSD_HEADS - GLA_RANK, GLA_KW), F32)], axis=1).astype(BF16)
    w_out_b = w_out.astype(BF16)
    wq_t = jnp.swapaxes(peer_wq, 1, 2).astype(BF16)
    half = PEER_QDIM // 2
    zeros_k = jnp.zeros((depth, PEER_NKEYS, half), F32)
    pair = jnp.concatenate([jnp.concatenate([peer_k1, zeros_k], axis=2),
                            jnp.concatenate([zeros_k, peer_k2], axis=2)], axis=1)
    kk_t = jnp.einsum('hg,lab->lhagb', jnp.eye(PEER_HEADS, dtype=F32), pair).reshape(
        depth, PEER_HEADS * 2 * PEER_NKEYS, PEER_HEADS * PEER_QDIM).astype(BF16)
    u_b = peer_u.astype(BF16)
    v_t = jnp.swapaxes(peer_v, 1, 2).astype(BF16)
    dtb = _pad_lanes(dt_bias, MISC_W)
    alog = _pad_lanes(a_log, MISC_W)
    dsk = _pad_lanes(d_skip, MISC_W)

    zero_gla = jnp.zeros((bp, GLA_DV, GLA_KW), F32)
    zero_ssm = jnp.zeros((bp, SSD_HEADS, SSD_HEADDIM, SSD_DSTATE), F32)
    zero_conv = jnp.zeros((bp, SSD_CONV - 1, CONV_DIM), F32)
    pad_rows = t_pad - tp - ts

    def gla_state_in(s):
        return jnp.transpose(s, (0, 3, 1, 2)).reshape(s.shape[0], GLA_DV, GLA_KW)

    def gla_state_out(s):
        return jnp.transpose(s.reshape(s.shape[0], GLA_DV, GLA_HEADS, GLA_DK), (0, 2, 3, 1))

    outs = [[] for _ in range(6)]
    for l in range(depth):
        qkvg, gk, zx, dtr = _in_proj(h, norm1_w[l][None], w_proj[l], w_gk[l], b_gk2[l][None], tm)
        gnw = gla_norm_w[l][None]
        og_p, sg_p = _gla(qkvg, gk, zero_gla, gnw, row0=0, nseq=bp, nblk=lp // PROMPT_ROWS,
                          nsub=PROMPT_ROWS // GLA_CHUNK, ch=GLA_CHUNK)
        og_s, sg_s = _gla(qkvg, gk, gla_state_in(state_gla[l]), gnw, row0=tp, nseq=bs, nblk=1,
                          nsub=1, ch=seq_s)
        ssd_w = (conv_w[l], conv_b[l][None], dtb[l][None], alog[l][None], dsk[l][None], ssd_norm_w[l][None])
        ys_p, ss_p, sc_p = _ssd(zx, dtr, zero_conv, zero_ssm, *ssd_w, row0=0, nseq=bp,
                                nblk=lp // PROMPT_ROWS, ch=PROMPT_ROWS)
        ys_s, ss_s, sc_s = _ssd(zx, dtr, state_conv[l], state_ssm[l], *ssd_w, row0=tp, nseq=bs,
                                nblk=1, ch=seq_s)
        og = jnp.concatenate([og_p, og_s, jnp.zeros((pad_rows, GLA_VW), F32)], axis=0)
        ys = jnp.concatenate([ys_p, ys_s, jnp.zeros((pad_rows, SSD_INNER), F32)], axis=0)
        h = _out_proj(h, og, ys, w_out_b[l], tm)
        xnt, s2, p2, c, p1 = _route(h, norm2_w[l][None], wq_t[l], kk_t[l], ROUTE_TILE)
        h = _peer(xnt, u_b[l], v_t[l], s2, p2, c, p1, h, tm, PEER_A_PER_STEP)
        for lst, val in zip(outs, (gla_state_out(sg_p), ss_p, sc_p, gla_state_out(sg_s), ss_s, sc_s)):
            lst.append(val)

    y = _final_norm(h, final_norm_w[None], tm)
    y_prompt = y[:tp].reshape(bp, lp, D_MODEL)[:, N_META:]
    y_sample = y[tp:tp + ts].reshape(bs, seq_s, D_MODEL)
    return (y_prompt, y_sample) + tuple(jnp.stack(o) for o in outs)
```

```python
import functools
import math

import jax
import jax.numpy as jnp
from jax import lax
from jax.experimental import pallas as pl
from jax.experimental.pallas import tpu as pltpu

F32 = jnp.float32
BF16 = jnp.bfloat16

D_MODEL = 1024
N_META = 16
EPS = 1e-6
GLA_HEADS = 4
GLA_DK = 64
GLA_DV = 128
GLA_KW = GLA_HEADS * GLA_DK
GLA_VW = GLA_HEADS * GLA_DV
GLA_RANK = 16
GLA_TAU = 16.0
SSD_HEADS = 8
SSD_HEADDIM = 64
SSD_INNER = SSD_HEADS * SSD_HEADDIM
SSD_GROUPS = 2
SSD_REP = SSD_HEADS // SSD_GROUPS
SSD_DSTATE = 128
SSD_CONV = 4
CONV_DIM = SSD_INNER + 2 * SSD_GROUPS * SSD_DSTATE
PEER_HEADS = 8
PEER_NKEYS = 128
PEER_QDIM = 128
PEER_TOPK = 16

LANES = 128
SUBLANES = 8
MISC_W = LANES
QKVG_W = 2 * GLA_KW + 2 * GLA_VW
ZX_W = SSD_INNER + CONV_DIM
PROJ_W = QKVG_W + ZX_W + MISC_W
PROMPT_ROWS = 48
GLA_CHUNK = 16
TOKEN_TILE = 512
ROUTE_TILE = 256
PEER_A_PER_STEP = 8
GATE_GROUP = 2
VMEM_LIMIT = 56 * 1024 * 1024

NT_DIMS = (((1,), (1,)), ((), ()))
TN_DIMS = (((0,), (0,)), ((), ()))


def _rms(x, w):
    return x * lax.rsqrt(jnp.mean(x * x, axis=-1, keepdims=True) + EPS) * w


def _softplus(x):
    return jnp.maximum(x, 0.0) + jnp.log1p(jnp.exp(-jnp.abs(x)))


def _silu(x):
    return x * (1.0 / (1.0 + jnp.exp(-x)))


def _gelu_tanh(x):
    c = math.sqrt(2.0 / math.pi)
    return 0.5 * x * (1.0 + jnp.tanh(c * (x + 0.044715 * (x * x * x))))


def _chunk_cumsum(x, ch):
    row = lax.broadcasted_iota(jnp.int32, x.shape, 0) % ch
    sh = 1
    while sh < ch:
        x = x + jnp.where(row >= sh, pltpu.roll(x, sh, 0), 0.0)
        sh *= 2
    return x


def _in_proj_kernel(h_ref, nw_ref, w_ref, wgk_ref, bgk_ref, qkvg_ref, gk_ref, zx_ref, dt_ref):
    xn = _rms(h_ref[...], nw_ref[...]).astype(BF16)
    y = jnp.dot(xn, w_ref[...], preferred_element_type=F32)
    qkvg_ref[:, 0:GLA_KW] = y[:, 0:GLA_KW] * (GLA_DK ** -0.5)
    qkvg_ref[:, GLA_KW:QKVG_W] = y[:, GLA_KW:QKVG_W]
    zx_ref[...] = y[:, QKVG_W:QKVG_W + ZX_W]
    misc = y[:, QKVG_W + ZX_W:PROJ_W]
    dt_ref[...] = misc
    pre = jnp.dot(misc.astype(BF16), wgk_ref[...], preferred_element_type=F32) + bgk_ref[...]
    log_sig = jnp.minimum(pre, 0.0) - jnp.log1p(jnp.exp(-jnp.abs(pre)))
    gk_ref[...] = log_sig / GLA_TAU


def _in_proj(h, nw, w, wgk, bgk, tm):
    t_pad = h.shape[0]
    row = lambda i: (i, 0)
    fixed = lambda i: (0, 0)
    return pl.pallas_call(
        _in_proj_kernel,
        grid=(t_pad // tm,),
        in_specs=[
            pl.BlockSpec((tm, D_MODEL), row),
            pl.BlockSpec((1, D_MODEL), fixed),
            pl.BlockSpec((D_MODEL, PROJ_W), fixed),
            pl.BlockSpec((MISC_W, GLA_KW), fixed),
            pl.BlockSpec((1, GLA_KW), fixed),
        ],
        out_specs=[
            pl.BlockSpec((tm, QKVG_W), row),
            pl.BlockSpec((tm, GLA_KW), row),
            pl.BlockSpec((tm, ZX_W), row),
            pl.BlockSpec((tm, MISC_W), row),
        ],
        out_shape=[
            jax.ShapeDtypeStruct((t_pad, QKVG_W), F32),
            jax.ShapeDtypeStruct((t_pad, GLA_KW), F32),
            jax.ShapeDtypeStruct((t_pad, ZX_W), F32),
            jax.ShapeDtypeStruct((t_pad, MISC_W), F32),
        ],
        compiler_params=pltpu.CompilerParams(
            dimension_semantics=("arbitrary",), vmem_limit_bytes=VMEM_LIMIT),
        name="in_proj",
    )(h, nw, w, wgk, bgk)


def _gla_kernel(qkvg_ref, gk_ref, s0_ref, nw_ref, o_ref, st_ref, *, nsub, ch):
    @pl.when(pl.program_id(1) == 0)
    def _():
        st_ref[...] = s0_ref[...]

    rb = nsub * ch
    q = qkvg_ref[:, 0:GLA_KW]
    k = qkvg_ref[:, GLA_KW:2 * GLA_KW]
    v = qkvg_ref[:, 2 * GLA_KW:2 * GLA_KW + GLA_VW]
    g = qkvg_ref[:, 2 * GLA_KW + GLA_VW:QKVG_W]
    b = _chunk_cumsum(gk_ref[...], ch)

    b3 = b.reshape(nsub, ch, GLA_KW)
    q3 = q.reshape(nsub, ch, GLA_KW)
    k3 = k.reshape(nsub, ch, GLA_KW)
    v3 = v.reshape(nsub, ch, GLA_VW)
    tix = lax.broadcasted_iota(jnp.int32, (nsub, ch, GLA_KW), 1)
    lane = lax.broadcasted_iota(jnp.int32, (nsub, ch, LANES), 2)
    low_half = lane < GLA_DK
    acc = [jnp.zeros((nsub, ch, GLA_DV), F32) for _ in range(GLA_HEADS)]
    for s in range(ch):
        diff = b3 - b3[:, s:s + 1, :]
        e = jnp.exp(jnp.where(tix >= s, diff, -jnp.inf))
        f = q3 * e * k3[:, s:s + 1, :]
        for h in range(GLA_HEADS):
            fh = f[:, :, (h // 2) * LANES:(h // 2 + 1) * LANES]
            mask = low_half if h % 2 == 0 else jnp.logical_not(low_half)
            col = jnp.sum(jnp.where(mask, fh, 0.0), axis=-1, keepdims=True)
            acc[h] = acc[h] + col * v3[:, s:s + 1, h * GLA_DV:(h + 1) * GLA_DV]

    nw = nw_ref[...]
    for i in range(nsub):
        r0 = i * ch
        bi = b[r0:r0 + ch]
        bl = bi[ch - 1:ch]
        qh = (q[r0:r0 + ch] * jnp.exp(bi)).astype(BF16)
        kh = (k[r0:r0 + ch] * jnp.exp(bl - bi)).astype(BF16)
        vi = v[r0:r0 + ch].astype(BF16)
        st = st_ref[0]
        stb = st.astype(BF16)
        kv = []
        for h in range(GLA_HEADS):
            ks = slice(h * GLA_DK, (h + 1) * GLA_DK)
            vs = slice(h * GLA_DV, (h + 1) * GLA_DV)
            o_inter = lax.dot_general(qh[:, ks], stb[:, ks], NT_DIMS, preferred_element_type=F32)
            kv.append(lax.dot_general(vi[:, vs], kh[:, ks], TN_DIMS, preferred_element_type=F32))
            o = acc[h][i] + o_inter
            gate = _silu(g[r0:r0 + ch, vs])
            o_ref[r0:r0 + ch, vs] = _rms(o, nw) * gate
        st_ref[0] = st * jnp.exp(bl) + jnp.concatenate(kv, axis=1)


def _gla(qkvg, gk, s0t, nw, *, row0, nseq, nblk, nsub, ch):
    rb = nsub * ch
    blk0 = row0 // rb
    row = lambda b, j: (blk0 + b * nblk + j, 0)
    seq = lambda b, j: (b, 0, 0)
    return pl.pallas_call(
        functools.partial(_gla_kernel, nsub=nsub, ch=ch),
        grid=(nseq, nblk),
        in_specs=[
            pl.BlockSpec((rb, QKVG_W), row),
            pl.BlockSpec((rb, GLA_KW), row),
            pl.BlockSpec((1, GLA_DV, GLA_KW), seq),
            pl.BlockSpec((1, GLA_DV), lambda b, j: (0, 0)),
        ],
        out_specs=[
            pl.BlockSpec((rb, GLA_VW), lambda b, j: (b * nblk + j, 0)),
            pl.BlockSpec((1, GLA_DV, GLA_KW), seq),
        ],
        out_shape=[
            jax.ShapeDtypeStruct((nseq * nblk * rb, GLA_VW), F32),
            jax.ShapeDtypeStruct((nseq, GLA_DV, GLA_KW), F32),
        ],
        compiler_params=pltpu.CompilerParams(
            dimension_semantics=("arbitrary", "arbitrary"), vmem_limit_bytes=VMEM_LIMIT),
        name="gla_scan",
    )(qkvg, gk, s0t, nw)


def _ssd_kernel(zx_ref, dt_ref, c0_ref, h0_ref, cw_ref, cb_ref, dtb_ref, alog_ref, dsk_ref, nw_ref,
                y_ref, hst_ref, cst_ref, carry_ref, *, ch):
    @pl.when(pl.program_id(1) == 0)
    def _():
        hst_ref[...] = h0_ref[...]
        carry_ref[...] = jnp.zeros_like(carry_ref)
        carry_ref[SUBLANES - (SSD_CONV - 1):SUBLANES, :] = c0_ref[0]

    z = zx_ref[:, 0:SSD_INNER]
    x = zx_ref[:, SSD_INNER:ZX_W]
    full = jnp.concatenate([carry_ref[...], x], axis=0)
    cw = cw_ref[...]
    base = SUBLANES - (SSD_CONV - 1)
    conv = cb_ref[...] + full[base:base + ch] * cw[0:1]
    for i in range(1, SSD_CONV):
        conv = conv + full[base + i:base + i + ch] * cw[i:i + 1]
    carry_ref[...] = full[ch:ch + SUBLANES]
    cst_ref[0] = full[ch + base:ch + SUBLANES]

    xc = _silu(conv)
    xs = xc[:, 0:SSD_INNER]
    bm = xc[:, SSD_INNER:SSD_INNER + SSD_GROUPS * SSD_DSTATE].astype(BF16)
    cm = xc[:, SSD_INNER + SSD_GROUPS * SSD_DSTATE:CONV_DIM].astype(BF16)

    dt = _softplus(dt_ref[...] + dtb_ref[...])
    a = -jnp.exp(alog_ref[...])
    cum = _chunk_cumsum(dt * a, ch)
    eye = (lax.broadcasted_iota(jnp.int32, (SUBLANES, LANES), 0)
           == lax.broadcasted_iota(jnp.int32, (SUBLANES, LANES), 1)).astype(F32)
    cum_t = lax.dot_general(eye, cum, NT_DIMS, precision=lax.Precision.HIGHEST,
                            preferred_element_type=F32)
    dt_t = lax.dot_general(eye, dt, NT_DIMS, precision=lax.Precision.HIGHEST,
                           preferred_element_type=F32)
    causal = (lax.broadcasted_iota(jnp.int32, (ch, ch), 0)
              >= lax.broadcasted_iota(jnp.int32, (ch, ch), 1))
    dsk = dsk_ref[...]
    ys = []
    for grp in range(SSD_GROUPS):
        bg = bm[:, grp * SSD_DSTATE:(grp + 1) * SSD_DSTATE]
        cg = cm[:, grp * SSD_DSTATE:(grp + 1) * SSD_DSTATE]
        cb = lax.dot_general(cg, bg, NT_DIMS, preferred_element_type=F32)
        for rep in range(SSD_REP):
            hh = grp * SSD_REP + rep
            cum_c = cum[:, hh:hh + 1]
            lmat = jnp.exp(jnp.where(causal, cum_c - cum_t[hh:hh + 1, :], -jnp.inf))
            m = cb * lmat * dt_t[hh:hh + 1, :]
            xh = xs[:, hh * SSD_HEADDIM:(hh + 1) * SSD_HEADDIM]
            y_intra = jnp.dot(m.astype(BF16), xh.astype(BF16), preferred_element_type=F32)
            cl = cum[ch - 1:ch, hh:hh + 1]
            xw = xh * (jnp.exp(cl - cum_c) * dt[:, hh:hh + 1])
            hs = hst_ref[0, hh]
            y_inter = lax.dot_general(cg, hs.astype(BF16), NT_DIMS, preferred_element_type=F32)
            hst_ref[0, hh] = jnp.exp(cl) * hs + lax.dot_general(
                xw.astype(BF16), bg, TN_DIMS, preferred_element_type=F32)
            ys.append(y_intra + y_inter * jnp.exp(cum_c) + dsk[:, hh:hh + 1] * xh)
    y = jnp.concatenate(ys, axis=1) * _silu(z)
    y_ref[...] = _rms(y, nw_ref[...])


def _ssd(zx, dtr, c0, h0, cw, cb, dtb, alog, dsk, nw, *, row0, nseq, nblk, ch):
    blk0 = row0 // ch
    row = lambda b, j: (blk0 + b * nblk + j, 0)
    fixed = lambda b, j: (0, 0)
    return pl.pallas_call(
        functools.partial(_ssd_kernel, ch=ch),
        grid=(nseq, nblk),
        in_specs=[
            pl.BlockSpec((ch, ZX_W), row),
            pl.BlockSpec((ch, MISC_W), row),
            pl.BlockSpec((1, SSD_CONV - 1, CONV_DIM), lambda b, j: (b, 0, 0)),
            pl.BlockSpec((1, SSD_HEADS, SSD_HEADDIM, SSD_DSTATE), lambda b, j: (b, 0, 0, 0)),
            pl.BlockSpec((SSD_CONV, CONV_DIM), fixed),
            pl.BlockSpec((1, CONV_DIM), fixed),
            pl.BlockSpec((1, MISC_W), fixed),
            pl.BlockSpec((1, MISC_W), fixed),
            pl.BlockSpec((1, MISC_W), fixed),
            pl.BlockSpec((1, SSD_INNER), fixed),
        ],
        out_specs=[
            pl.BlockSpec((ch, SSD_INNER), lambda b, j: (b * nblk + j, 0)),
            pl.BlockSpec((1, SSD_HEADS, SSD_HEADDIM, SSD_DSTATE), lambda b, j: (b, 0, 0, 0)),
            pl.BlockSpec((1, SSD_CONV - 1, CONV_DIM), lambda b, j: (b, 0, 0)),
        ],
        out_shape=[
            jax.ShapeDtypeStruct((nseq * nblk * ch, SSD_INNER), F32),
            jax.ShapeDtypeStruct((nseq, SSD_HEADS, SSD_HEADDIM, SSD_DSTATE), F32),
            jax.ShapeDtypeStruct((nseq, SSD_CONV - 1, CONV_DIM), F32),
        ],
        scratch_shapes=[pltpu.VMEM((SUBLANES, CONV_DIM), F32)],
        compiler_params=pltpu.CompilerParams(
            dimension_semantics=("arbitrary", "arbitrary"), vmem_limit_bytes=VMEM_LIMIT),
        name="ssd_scan",
    )(zx, dtr, c0, h0, cw, cb, dtb, alog, dsk, nw)


def _out_proj_kernel(h_ref, og_ref, ys_ref, w_ref, o_ref):
    acc = jnp.dot(og_ref[...].astype(BF16), w_ref[0:GLA_VW, :], preferred_element_type=F32)
    acc = acc + jnp.dot(ys_ref[...].astype(BF16), w_ref[GLA_VW:GLA_VW + SSD_INNER, :],
                        preferred_element_type=F32)
    o_ref[...] = h_ref[...] + acc


def _out_proj(h, og, ys, w, tm):
    t_pad = h.shape[0]
    row = lambda i: (i, 0)
    return pl.pallas_call(
        _out_proj_kernel,
        grid=(t_pad // tm,),
        in_specs=[
            pl.BlockSpec((tm, D_MODEL), row),
            pl.BlockSpec((tm, GLA_VW), row),
            pl.BlockSpec((tm, SSD_INNER), row),
            pl.BlockSpec((GLA_VW + SSD_INNER, D_MODEL), lambda i: (0, 0)),
        ],
        out_specs=pl.BlockSpec((tm, D_MODEL), row),
        out_shape=jax.ShapeDtypeStruct((t_pad, D_MODEL), F32),
        compiler_params=pltpu.CompilerParams(
            dimension_semantics=("arbitrary",), vmem_limit_bytes=VMEM_LIMIT),
        name="out_proj",
    )(h, og, ys, w)


N_EXTRACT = PEER_TOPK + 1
VAL_ROWS = 24
NOT_RANKED = 255.0


def _top_values(s, tm, want_rank=False):
    rowi = lax.broadcasted_iota(jnp.int32, (VAL_ROWS, tm), 0)
    vals = jnp.full((VAL_ROWS, tm), -jnp.inf, F32)
    rank = jnp.full(s.shape, NOT_RANKED, F32) if want_rank else None
    for i in range(N_EXTRACT):
        m = jnp.max(s, axis=0, keepdims=True)
        vals = jnp.where(rowi == i, m, vals)
        hit = s == m
        if want_rank:
            rank = jnp.where(hit, float(i), rank)
        s = jnp.where(hit, -jnp.inf, s)
    return (vals, rank) if want_rank else vals


def _route_kernel(h_ref, nw_ref, wqt_ref, kkt_ref, xnt_ref, r2_ref, p2_ref, n_ref, p1_ref, *, tm):
    xn = _rms(h_ref[...], nw_ref[...])
    xnb = xn.astype(BF16)
    xnt_ref[...] = xn.T.astype(BF16)
    qt = lax.dot_general(wqt_ref[...], xnb, NT_DIMS, preferred_element_type=F32)
    st = jnp.dot(kkt_ref[...], qt.astype(BF16), preferred_element_type=F32)
    row8 = lax.broadcasted_iota(jnp.int32, (SUBLANES, tm), 0)
    row24 = lax.broadcasted_iota(jnp.int32, (VAL_ROWS, tm), 0)
    for h in range(PEER_HEADS):
        r1 = h * 2 * PEER_NKEYS
        s1 = st[r1:r1 + PEER_NKEYS]
        s2 = st[r1 + PEER_NKEYS:r1 + 2 * PEER_NKEYS]
        v1 = _top_values(s1, tm)
        v2, rank2 = _top_values(s2, tm, want_rank=True)
        pieces = [jnp.where(row24 < N_EXTRACT, v1[0:1] + v2, -jnp.inf)]
        for i in range(1, SUBLANES):
            lim = N_EXTRACT // (i + 1)
            pieces.append(jnp.where(row8 < lim, v1[i:i + 1] + v2[0:SUBLANES], -jnp.inf))
        pieces.append(v1[SUBLANES:VAL_ROWS] + v2[0:1])
        cand = jnp.concatenate(pieces, axis=0)
        best = _top_values(cand, tm)
        thr = 0.5 * (best[PEER_TOPK - 1:PEER_TOPK] + best[PEER_TOPK:PEER_TOPK + 1])
        top = best[0:1]
        zsum = jnp.sum(jnp.where(cand >= thr, jnp.exp(cand - top), 0.0), axis=0, keepdims=True)
        c = thr - s1
        n = jnp.zeros_like(c)
        for j in range(PEER_TOPK):
            n = n + jnp.where(v2[j:j + 1] >= c, 1.0, 0.0)
        rows = slice(h * PEER_NKEYS, (h + 1) * PEER_NKEYS)
        r2_ref[rows, :] = rank2.astype(BF16)
        p2_ref[rows, :] = (jnp.exp(s2 - v2[0:1]) / zsum).astype(BF16)
        n_ref[rows, :] = n
        p1_ref[rows, :] = jnp.exp(s1 - v1[0:1])


def _route(h, nw, wqt, kkt, tm):
    t_pad = h.shape[0]
    hk = PEER_HEADS * PEER_NKEYS
    col = lambda i: (0, i)
    fixed = lambda i: (0, 0)
    return pl.pallas_call(
        functools.partial(_route_kernel, tm=tm),
        grid=(t_pad // tm,),
        in_specs=[
            pl.BlockSpec((tm, D_MODEL), lambda i: (i, 0)),
            pl.BlockSpec((1, D_MODEL), fixed),
            pl.BlockSpec((PEER_HEADS * PEER_QDIM, D_MODEL), fixed),
            pl.BlockSpec((2 * hk, PEER_HEADS * PEER_QDIM), fixed),
        ],
        out_specs=[pl.BlockSpec((D_MODEL, tm), col)] + [pl.BlockSpec((hk, tm), col)] * 4,
        out_shape=[jax.ShapeDtypeStruct((D_MODEL, t_pad), BF16)]
        + [jax.ShapeDtypeStruct((hk, t_pad), dt) for dt in (BF16, BF16, F32, F32)],
        compiler_params=pltpu.CompilerParams(
            dimension_semantics=("arbitrary",), vmem_limit_bytes=VMEM_LIMIT),
        name="peer_route",
    )(h, nw, wqt, kkt)


def _peer_kernel(xnt_ref, u_ref, vt_ref, r2_in_ref, p2_in_ref, n_ref, p1_ref, h_ref, o_ref,
                 acc_ref, hid_ref, w_ref, r2_ref, p2_ref, *, na, tm):
    e = pl.program_id(1)

    @pl.when(e == 0)
    def _():
        acc_ref[...] = jnp.zeros_like(acc_ref)
        r2_ref[...] = r2_in_ref[...]
        p2_ref[...] = p2_in_ref[...]

    hid_ref[...] = jnp.dot(u_ref[...], xnt_ref[...], preferred_element_type=F32).astype(BF16)
    assert na == SUBLANES
    a0 = pl.multiple_of(e * na, SUBLANES)
    for tc in range(tm // LANES):
        cols = slice(tc * LANES, (tc + 1) * LANES)
        for al0 in range(0, na, GATE_GROUP):
            gates = [None] * GATE_GROUP
            for h in range(PEER_HEADS):
                rows = slice(h * PEER_NKEYS, (h + 1) * PEER_NKEYS)
                r2 = r2_ref[rows, cols]
                p2 = p2_ref[rows, cols]
                n_blk = n_ref[pl.ds(h * PEER_NKEYS + a0, na), cols]
                p_blk = p1_ref[pl.ds(h * PEER_NKEYS + a0, na), cols]
                for k in range(GATE_GROUP):
                    al = al0 + k
                    n_row = n_blk[al:al + 1].astype(BF16)
                    p_row = p_blk[al:al + 1].astype(BF16)
                    term = jnp.where(r2 < n_row, p2, 0) * p_row
                    gates[k] = term if gates[k] is None else gates[k] + term
            for k in range(GATE_GROUP):
                erows = slice((al0 + k) * PEER_NKEYS, (al0 + k + 1) * PEER_NKEYS)
                w_ref[erows, cols] = _gelu_tanh(hid_ref[erows, cols]) * gates[k]
    acc_ref[...] += jnp.dot(vt_ref[...], w_ref[...], preferred_element_type=F32)

    @pl.when(e == pl.num_programs(1) - 1)
    def _():
        o_ref[...] = h_ref[...] + acc_ref[...].T


def _peer(xnt, u, vt, r2, p2, n, p1, h, tm, na):
    t_pad = h.shape[0]
    hk = PEER_HEADS * PEER_NKEYS
    te = na * PEER_NKEYS
    tok = lambda i, e: (0, i)
    return pl.pallas_call(
        functools.partial(_peer_kernel, na=na, tm=tm),
        grid=(t_pad // tm, PEER_NKEYS // na),
        in_specs=[
            pl.BlockSpec((D_MODEL, tm), tok),
            pl.BlockSpec((te, D_MODEL), lambda i, e: (e, 0)),
            pl.BlockSpec((D_MODEL, te), lambda i, e: (0, e)),
            pl.BlockSpec((hk, tm), tok),
            pl.BlockSpec((hk, tm), tok),
            pl.BlockSpec((hk, tm), tok),
            pl.BlockSpec((hk, tm), tok),
            pl.BlockSpec((tm, D_MODEL), lambda i, e: (i, 0)),
        ],
        out_specs=pl.BlockSpec((tm, D_MODEL), lambda i, e: (i, 0)),
        out_shape=jax.ShapeDtypeStruct((t_pad, D_MODEL), F32),
        scratch_shapes=[pltpu.VMEM((D_MODEL, tm), F32), pltpu.VMEM((te, tm), BF16),
                        pltpu.VMEM((te, tm), BF16), pltpu.VMEM((hk, tm), BF16),
                        pltpu.VMEM((hk, tm), BF16)],
        compiler_params=pltpu.CompilerParams(
            dimension_semantics=("arbitrary", "arbitrary"), vmem_limit_bytes=VMEM_LIMIT),
        name="peer_experts",
    )(xnt, u, vt, r2, p2, n, p1, h)


def _final_norm_kernel(h_ref, w_ref, o_ref):
    o_ref[...] = _rms(h_ref[...], w_ref[...])


def _final_norm(h, w, tm):
    t_pad = h.shape[0]
    return pl.pallas_call(
        _final_norm_kernel,
        grid=(t_pad // tm,),
        in_specs=[pl.BlockSpec((tm, D_MODEL), lambda i: (i, 0)),
                  pl.BlockSpec((1, D_MODEL), lambda i: (0, 0))],
        out_specs=pl.BlockSpec((tm, D_MODEL), lambda i: (i, 0)),
        out_shape=jax.ShapeDtypeStruct((t_pad, D_MODEL), F32),
        name="final_norm",
    )(h, w)


def _pad_lanes(x, width):
    return jnp.pad(x, ((0, 0), (0, width - x.shape[-1])))


def kernel(x_prompt, x_sample, state_gla, state_ssm, state_conv, meta_tokens, norm1_w, w_in, w_gk2, b_gk2, gla_norm_w, conv_w, conv_b, dt_bias, a_log, d_skip, ssd_norm_w, w_out, norm2_w, peer_wq, peer_k1, peer_k2, peer_u, peer_v, final_norm_w):
    bp, seq_p, _ = x_prompt.shape
    bs, seq_s, _ = x_sample.shape
    depth = w_in.shape[0]
    lp = seq_p + N_META
    tp, ts = bp * lp, bs * seq_s
    tm = TOKEN_TILE
    t_pad = -(-(tp + ts) // tm) * tm
    assert lp % PROMPT_ROWS == 0 and seq_s % SUBLANES == 0 and tp % seq_s == 0
    assert SSD_CONV - 1 <= seq_s and t_pad % ROUTE_TILE == 0

    hp = jnp.concatenate([jnp.broadcast_to(meta_tokens[None], (bp, N_META, D_MODEL)), x_prompt], axis=1)
    h = jnp.concatenate([hp.reshape(tp, D_MODEL), x_sample.reshape(ts, D_MODEL),
                         jnp.zeros((t_pad - tp - ts, D_MODEL), F32)], axis=0)

    offs = [0, GLA_KW, 2 * GLA_KW, 2 * GLA_KW + GLA_VW, QKVG_W, QKVG_W + GLA_RANK,
            QKVG_W + GLA_RANK + SSD_INNER, QKVG_W + GLA_RANK + SSD_INNER + CONV_DIM]
    w_flr = w_in[:, :, offs[4]:offs[5]]
    w_zx = w_in[:, :, offs[5]:offs[7]]
    w_dt = w_in[:, :, offs[7]:offs[7] + SSD_HEADS]
    w_misc = jnp.concatenate(
        [w_dt, w_flr, jnp.zeros((depth, D_MODEL, MISC_W - SSD_HEADS - GLA_RANK), F32)], axis=-1)
    w_proj = jnp.concatenate([w_in[:, :, 0:QKVG_W], w_zx, w_misc], axis=-1).astype(BF16)
    w_gk = jnp.concatenate(
        [jnp.zeros((depth, SSD_HEADS, GLA_KW), F32), w_gk2,
         jnp.zeros((depth, MISC_W - SSD_HEADS - GLA_RANK, GLA_KW), F32)], axis=1).astype(BF16)
    w_out_b = w_out.astype(BF16)
    wq_t = jnp.swapaxes(peer_wq, 1, 2).astype(BF16)
    half = PEER_QDIM // 2
    zeros_k = jnp.zeros((depth, PEER_NKEYS, half), F32)
    pair = jnp.concatenate([jnp.concatenate([peer_k1, zeros_k], axis=2),
                            jnp.concatenate([zeros_k, peer_k2], axis=2)], axis=1)
    kk_t = jnp.einsum('hg,lab->lhagb', jnp.eye(PEER_HEADS, dtype=F32), pair).reshape(
        depth, PEER_HEADS * 2 * PEER_NKEYS, PEER_HEADS * PEER_QDIM).astype(BF16)
    u_b = peer_u.astype(BF16)
    v_t = jnp.swapaxes(peer_v, 1, 2).astype(BF16)
    dtb = _pad_lanes(dt_bias, MISC_W)
    alog = _pad_lanes(a_log, MISC_W)
    dsk = _pad_lanes(d_skip, MISC_W)

    zero_gla = jnp.zeros((bp, GLA_DV, GLA_KW), F32)
    zero_ssm = jnp.zeros((bp, SSD_HEADS, SSD_HEADDIM, SSD_DSTATE), F32)
    zero_conv = jnp.zeros((bp, SSD_CONV - 1, CONV_DIM), F32)
    pad_rows = t_pad - tp - ts

    def gla_state_in(s):
        return jnp.transpose(s, (0, 3, 1, 2)).reshape(s.shape[0], GLA_DV, GLA_KW)

    def gla_state_out(s):
        return jnp.transpose(s.reshape(s.shape[0], GLA_DV, GLA_HEADS, GLA_DK), (0, 2, 3, 1))

    outs = [[] for _ in range(6)]
    for l in range(depth):
        qkvg, gk, zx, dtr = _in_proj(h, norm1_w[l][None], w_proj[l], w_gk[l], b_gk2[l][None], tm)
        gnw = gla_norm_w[l][None]
        og_p, sg_p = _gla(qkvg, gk, zero_gla, gnw, row0=0, nseq=bp, nblk=lp // PROMPT_ROWS,
                          nsub=PROMPT_ROWS // GLA_CHUNK, ch=GLA_CHUNK)
        og_s, sg_s = _gla(qkvg, gk, gla_state_in(state_gla[l]), gnw, row0=tp, nseq=bs, nblk=1,
                          nsub=1, ch=seq_s)
        ssd_w = (conv_w[l], conv_b[l][None], dtb[l][None], alog[l][None], dsk[l][None], ssd_norm_w[l][None])
        ys_p, ss_p, sc_p = _ssd(zx, dtr, zero_conv, zero_ssm, *ssd_w, row0=0, nseq=bp,
                                nblk=lp // PROMPT_ROWS, ch=PROMPT_ROWS)
        ys_s, ss_s, sc_s = _ssd(zx, dtr, state_conv[l], state_ssm[l], *ssd_w, row0=tp, nseq=bs,
                                nblk=1, ch=seq_s)
        og = jnp.concatenate([og_p, og_s, jnp.zeros((pad_rows, GLA_VW), F32)], axis=0)
        ys = jnp.concatenate([ys_p, ys_s, jnp.zeros((pad_rows, SSD_INNER), F32)], axis=0)
        h = _out_proj(h, og, ys, w_out_b[l], tm)
        xnt, r2, p2, n, p1 = _route(h, norm2_w[l][None], wq_t[l], kk_t[l], ROUTE_TILE)
        h = _peer(xnt, u_b[l], v_t[l], r2, p2, n, p1, h, tm, PEER_A_PER_STEP)
        for lst, val in zip(outs, (gla_state_out(sg_p), ss_p, sc_p, gla_state_out(sg_s), ss_s, sc_s)):
            lst.append(val)

    y = _final_norm(h, final_norm_w[None], tm)
    y_prompt = y[:tp].reshape(bp, lp, D_MODEL)[:, N_META:]
    y_sample = y[tp:tp + ts].reshape(bs, seq_s, D_MODEL)
    return (y_prompt, y_sample) + tuple(jnp.stack(o) for o in outs)
```

```python
import functools
import math

import jax
import jax.numpy as jnp
from jax import lax
from jax.experimental import pallas as pl
from jax.experimental.pallas import tpu as pltpu

F32 = jnp.float32
BF16 = jnp.bfloat16

D_MODEL = 1024
N_META = 16
EPS = 1e-6
GLA_HEADS = 4
GLA_DK = 64
GLA_DV = 128
GLA_KW = GLA_HEADS * GLA_DK
GLA_VW = GLA_HEADS * GLA_DV
GLA_RANK = 16
GLA_TAU = 16.0
SSD_HEADS = 8
SSD_HEADDIM = 64
SSD_INNER = SSD_HEADS * SSD_HEADDIM
SSD_GROUPS = 2
SSD_REP = SSD_HEADS // SSD_GROUPS
SSD_DSTATE = 128
SSD_CONV = 4
CONV_DIM = SSD_INNER + 2 * SSD_GROUPS * SSD_DSTATE
PEER_HEADS = 8
PEER_NKEYS = 128
PEER_QDIM = 128
PEER_TOPK = 16

LANES = 128
SUBLANES = 8
MISC_W = LANES
QKVG_W = 2 * GLA_KW + 2 * GLA_VW
ZX_W = SSD_INNER + CONV_DIM
PROJ_W = QKVG_W + ZX_W + MISC_W
MIX_W = GLA_VW + SSD_INNER
PROMPT_ROWS = 48
SAMPLE_SEQS_PER_STEP = 4
GLA_CHUNK = 16
TOKEN_TILE = 512
ROUTE_TILE = 256
PEER_A_PER_STEP = 8
WEIGHT_PARTS = 4
GATE_GROUP = 2
VMEM_LIMIT = 56 * 1024 * 1024

NT_DIMS = (((1,), (1,)), ((), ()))
TN_DIMS = (((0,), (0,)), ((), ()))


def _rms(x, w):
    return x * lax.rsqrt(jnp.mean(x * x, axis=-1, keepdims=True) + EPS) * w


def _softplus(x):
    return jnp.maximum(x, 0.0) + jnp.log1p(jnp.exp(-jnp.abs(x)))


def _silu(x):
    return x * (1.0 / (1.0 + jnp.exp(-x)))


def _gelu_tanh(x):
    c = math.sqrt(2.0 / math.pi)
    return 0.5 * x * (1.0 + jnp.tanh(c * (x + 0.044715 * (x * x * x))))


def _chunk_cumsum(x, ch):
    row = lax.broadcasted_iota(jnp.int32, x.shape, 0) % ch
    sh = 1
    while sh < ch:
        x = x + jnp.where(row >= sh, pltpu.roll(x, sh, 0), 0.0)
        sh *= 2
    return x


def _in_proj_kernel(h_ref, nw_ref, w_ref, wgk_ref, bgk_ref, qkvg_ref, gk_ref, zx_ref, dt_ref):
    xn = _rms(h_ref[...], nw_ref[...]).astype(BF16)
    y = jnp.dot(xn, w_ref[...], preferred_element_type=F32)
    qkvg_ref[:, 0:GLA_KW] = y[:, 0:GLA_KW] * (GLA_DK ** -0.5)
    qkvg_ref[:, GLA_KW:QKVG_W] = y[:, GLA_KW:QKVG_W]
    zx_ref[...] = y[:, QKVG_W:QKVG_W + ZX_W]
    misc = y[:, QKVG_W + ZX_W:PROJ_W]
    dt_ref[...] = misc
    pre = jnp.dot(misc.astype(BF16), wgk_ref[...], preferred_element_type=F32) + bgk_ref[...]
    log_sig = jnp.minimum(pre, 0.0) - jnp.log1p(jnp.exp(-jnp.abs(pre)))
    gk_ref[...] = log_sig / GLA_TAU


def _in_proj(h, nw, w, wgk, bgk, tm):
    t_pad = h.shape[0]
    row = lambda i: (i, 0)
    fixed = lambda i: (0, 0)
    return pl.pallas_call(
        _in_proj_kernel,
        grid=(t_pad // tm,),
        in_specs=[
            pl.BlockSpec((tm, D_MODEL), row),
            pl.BlockSpec((1, D_MODEL), fixed),
            pl.BlockSpec((D_MODEL, PROJ_W), fixed),
            pl.BlockSpec((MISC_W, GLA_KW), fixed),
            pl.BlockSpec((1, GLA_KW), fixed),
        ],
        out_specs=[
            pl.BlockSpec((tm, QKVG_W), row),
            pl.BlockSpec((tm, GLA_KW), row),
            pl.BlockSpec((tm, ZX_W), row),
            pl.BlockSpec((tm, MISC_W), row),
        ],
        out_shape=[
            jax.ShapeDtypeStruct((t_pad, QKVG_W), F32),
            jax.ShapeDtypeStruct((t_pad, GLA_KW), F32),
            jax.ShapeDtypeStruct((t_pad, ZX_W), F32),
            jax.ShapeDtypeStruct((t_pad, MISC_W), F32),
        ],
        compiler_params=pltpu.CompilerParams(
            dimension_semantics=("arbitrary",), vmem_limit_bytes=VMEM_LIMIT),
        name="in_proj",
    )(h, nw, w, wgk, bgk)


def _gla_part(qkvg_ref, gk_ref, nw_ref, mix_ref, st_ref, *, nsq, nsub, ch):
    nchunk = nsq * nsub
    q = qkvg_ref[:, 0:GLA_KW]
    k = qkvg_ref[:, GLA_KW:2 * GLA_KW]
    v = qkvg_ref[:, 2 * GLA_KW:2 * GLA_KW + GLA_VW]
    g = qkvg_ref[:, 2 * GLA_KW + GLA_VW:QKVG_W]
    b = _chunk_cumsum(gk_ref[...], ch)

    b3 = b.reshape(nchunk, ch, GLA_KW)
    q3 = q.reshape(nchunk, ch, GLA_KW)
    k3 = k.reshape(nchunk, ch, GLA_KW)
    v3 = v.reshape(nchunk, ch, GLA_VW)
    tix = lax.broadcasted_iota(jnp.int32, (nchunk, ch, GLA_KW), 1)
    lane = lax.broadcasted_iota(jnp.int32, (nchunk, ch, LANES), 2)
    low_half = lane < GLA_DK
    acc = [jnp.zeros((nchunk, ch, GLA_DV), F32) for _ in range(GLA_HEADS)]
    for s in range(ch):
        diff = b3 - b3[:, s:s + 1, :]
        e = jnp.exp(jnp.where(tix >= s, diff, -jnp.inf))
        f = q3 * e * k3[:, s:s + 1, :]
        for h in range(GLA_HEADS):
            fh = f[:, :, (h // 2) * LANES:(h // 2 + 1) * LANES]
            mask = low_half if h % 2 == 0 else jnp.logical_not(low_half)
            col = jnp.sum(jnp.where(mask, fh, 0.0), axis=-1, keepdims=True)
            acc[h] = acc[h] + col * v3[:, s:s + 1, h * GLA_DV:(h + 1) * GLA_DV]

    nw = nw_ref[...]
    for i in range(nchunk):
        sq = i // nsub
        r0 = i * ch
        bi = b[r0:r0 + ch]
        bl = bi[ch - 1:ch]
        qh = (q[r0:r0 + ch] * jnp.exp(bi)).astype(BF16)
        kh = (k[r0:r0 + ch] * jnp.exp(bl - bi)).astype(BF16)
        vi = v[r0:r0 + ch].astype(BF16)
        st = st_ref[sq]
        stb = st.astype(BF16)
        kv = []
        for h in range(GLA_HEADS):
            ks = slice(h * GLA_DK, (h + 1) * GLA_DK)
            vs = slice(h * GLA_DV, (h + 1) * GLA_DV)
            o_inter = lax.dot_general(qh[:, ks], stb[:, ks], NT_DIMS, preferred_element_type=F32)
            kv.append(lax.dot_general(vi[:, vs], kh[:, ks], TN_DIMS, preferred_element_type=F32))
            o = acc[h][i] + o_inter
            gate = _silu(g[r0:r0 + ch, vs])
            mix_ref[r0:r0 + ch, vs] = _rms(o, nw) * gate
        st_ref[sq] = st * jnp.exp(bl) + jnp.concatenate(kv, axis=1)


def _ssd_part(zx_ref, dt_ref, cw_ref, cb_ref, dtb_ref, alog_ref, dsk_ref, nw_ref,
              mix_ref, hst_ref, cst_ref, carry_ref, *, sq, ch):
    rs = slice(sq * ch, (sq + 1) * ch)
    z = zx_ref[rs, 0:SSD_INNER]
    x = zx_ref[rs, SSD_INNER:ZX_W]
    full = jnp.concatenate([carry_ref[sq], x], axis=0)
    cw = cw_ref[...]
    base = SUBLANES - (SSD_CONV - 1)
    conv = cb_ref[...] + full[base:base + ch] * cw[0:1]
    for i in range(1, SSD_CONV):
        conv = conv + full[base + i:base + i + ch] * cw[i:i + 1]
    carry_ref[sq] = full[ch:ch + SUBLANES]
    cst_ref[sq] = full[ch + base:ch + SUBLANES]

    xc = _silu(conv)
    xs = xc[:, 0:SSD_INNER]
    bm = xc[:, SSD_INNER:SSD_INNER + SSD_GROUPS * SSD_DSTATE].astype(BF16)
    cm = xc[:, SSD_INNER + SSD_GROUPS * SSD_DSTATE:CONV_DIM].astype(BF16)

    dt = _softplus(dt_ref[rs, :] + dtb_ref[...])
    a = -jnp.exp(alog_ref[...])
    cum = _chunk_cumsum(dt * a, ch)
    eye = (lax.broadcasted_iota(jnp.int32, (SUBLANES, LANES), 0)
           == lax.broadcasted_iota(jnp.int32, (SUBLANES, LANES), 1)).astype(F32)
    cum_t = lax.dot_general(eye, cum, NT_DIMS, precision=lax.Precision.HIGHEST,
                            preferred_element_type=F32)
    dt_t = lax.dot_general(eye, dt, NT_DIMS, precision=lax.Precision.HIGHEST,
                           preferred_element_type=F32)
    causal = (lax.broadcasted_iota(jnp.int32, (ch, ch), 0)
              >= lax.broadcasted_iota(jnp.int32, (ch, ch), 1))
    dsk = dsk_ref[...]
    ys = []
    for grp in range(SSD_GROUPS):
        bg = bm[:, grp * SSD_DSTATE:(grp + 1) * SSD_DSTATE]
        cg = cm[:, grp * SSD_DSTATE:(grp + 1) * SSD_DSTATE]
        cb = lax.dot_general(cg, bg, NT_DIMS, preferred_element_type=F32)
        for rep in range(SSD_REP):
            hh = grp * SSD_REP + rep
            cum_c = cum[:, hh:hh + 1]
            lmat = jnp.exp(jnp.where(causal, cum_c - cum_t[hh:hh + 1, :], -jnp.inf))
            m = cb * lmat * dt_t[hh:hh + 1, :]
            xh = xs[:, hh * SSD_HEADDIM:(hh + 1) * SSD_HEADDIM]
            y_intra = jnp.dot(m.astype(BF16), xh.astype(BF16), preferred_element_type=F32)
            cl = cum[ch - 1:ch, hh:hh + 1]
            xw = xh * (jnp.exp(cl - cum_c) * dt[:, hh:hh + 1])
            hs = hst_ref[sq, hh]
            y_inter = lax.dot_general(cg, hs.astype(BF16), NT_DIMS, preferred_element_type=F32)
            hst_ref[sq, hh] = jnp.exp(cl) * hs + lax.dot_general(
                xw.astype(BF16), bg, TN_DIMS, preferred_element_type=F32)
            ys.append(y_intra + y_inter * jnp.exp(cum_c) + dsk[:, hh:hh + 1] * xh)
    y = jnp.concatenate(ys, axis=1) * _silu(z)
    mix_ref[rs, GLA_VW:MIX_W] = _rms(y, nw_ref[...])


def _mixer_kernel(qkvg_ref, gk_ref, zx_ref, dt_ref, s0_ref, h0_ref, c0_ref, gnw_ref,
                  cw_ref, cb_ref, dtb_ref, alog_ref, dsk_ref, snw_ref,
                  mix_ref, st_ref, hst_ref, cst_ref, carry_ref, *, nsq, nsub, ch):
    @pl.when(pl.program_id(1) == 0)
    def _():
        st_ref[...] = s0_ref[...]
        hst_ref[...] = h0_ref[...]
        carry_ref[...] = jnp.zeros_like(carry_ref)
        carry_ref[:, SUBLANES - (SSD_CONV - 1):SUBLANES, :] = c0_ref[...]

    _gla_part(qkvg_ref, gk_ref, gnw_ref, mix_ref, st_ref, nsq=nsq, nsub=nsub, ch=ch)
    for sq in range(nsq):
        _ssd_part(zx_ref, dt_ref, cw_ref, cb_ref, dtb_ref, alog_ref, dsk_ref, snw_ref,
                  mix_ref, hst_ref, cst_ref, carry_ref, sq=sq, ch=nsub * ch)


def _mixers(qkvg, gk, zx, dtr, s0t, h0, c0, gnw, cw, cb, dtb, alog, dsk, snw,
            *, row0, nseq, nsq, nblk, nsub, ch):
    assert nseq % nsq == 0 and (nsq == 1 or nblk == 1)
    rb = nsq * nsub * ch
    blk0 = row0 // rb
    row = lambda b, j: (blk0 + b * nblk + j, 0)
    fixed = lambda b, j: (0, 0)
    seq3 = lambda b, j: (b, 0, 0)
    seq4 = lambda b, j: (b, 0, 0, 0)
    return pl.pallas_call(
        functools.partial(_mixer_kernel, nsq=nsq, nsub=nsub, ch=ch),
        grid=(nseq // nsq, nblk),
        in_specs=[
            pl.BlockSpec((rb, QKVG_W), row),
            pl.BlockSpec((rb, GLA_KW), row),
            pl.BlockSpec((rb, ZX_W), row),
            pl.BlockSpec((rb, MISC_W), row),
            pl.BlockSpec((nsq, GLA_DV, GLA_KW), seq3),
            pl.BlockSpec((nsq, SSD_HEADS, SSD_HEADDIM, SSD_DSTATE), seq4),
            pl.BlockSpec((nsq, SSD_CONV - 1, CONV_DIM), seq3),
            pl.BlockSpec((1, GLA_DV), fixed),
            pl.BlockSpec((SSD_CONV, CONV_DIM), fixed),
            pl.BlockSpec((1, CONV_DIM), fixed),
            pl.BlockSpec((1, MISC_W), fixed),
            pl.BlockSpec((1, MISC_W), fixed),
            pl.BlockSpec((1, MISC_W), fixed),
            pl.BlockSpec((1, SSD_INNER), fixed),
        ],
        out_specs=[
            pl.BlockSpec((rb, MIX_W), lambda b, j: (b * nblk + j, 0)),
            pl.BlockSpec((nsq, GLA_DV, GLA_KW), seq3),
            pl.BlockSpec((nsq, SSD_HEADS, SSD_HEADDIM, SSD_DSTATE), seq4),
            pl.BlockSpec((nsq, SSD_CONV - 1, CONV_DIM), seq3),
        ],
        out_shape=[
            jax.ShapeDtypeStruct((nseq * nblk * nsub * ch, MIX_W), F32),
            jax.ShapeDtypeStruct((nseq, GLA_DV, GLA_KW), F32),
            jax.ShapeDtypeStruct((nseq, SSD_HEADS, SSD_HEADDIM, SSD_DSTATE), F32),
            jax.ShapeDtypeStruct((nseq, SSD_CONV - 1, CONV_DIM), F32),
        ],
        scratch_shapes=[pltpu.VMEM((nsq, SUBLANES, CONV_DIM), F32)],
        compiler_params=pltpu.CompilerParams(
            dimension_semantics=("arbitrary", "arbitrary"), vmem_limit_bytes=VMEM_LIMIT),
        name="token_mixers",
    )(qkvg, gk, zx, dtr, s0t, h0, c0, gnw, cw, cb, dtb, alog, dsk, snw)


def _out_proj_kernel(h_ref, mix_ref, w_ref, o_ref):
    o_ref[...] = h_ref[...] + jnp.dot(mix_ref[...].astype(BF16), w_ref[...],
                                      preferred_element_type=F32)


def _out_proj(h, mix, w, tm):
    t_pad = h.shape[0]
    row = lambda i: (i, 0)
    return pl.pallas_call(
        _out_proj_kernel,
        grid=(t_pad // tm,),
        in_specs=[
            pl.BlockSpec((tm, D_MODEL), row),
            pl.BlockSpec((tm, MIX_W), row),
            pl.BlockSpec((MIX_W, D_MODEL), lambda i: (0, 0)),
        ],
        out_specs=pl.BlockSpec((tm, D_MODEL), row),
        out_shape=jax.ShapeDtypeStruct((t_pad, D_MODEL), F32),
        compiler_params=pltpu.CompilerParams(
            dimension_semantics=("arbitrary",), vmem_limit_bytes=VMEM_LIMIT),
        name="out_proj",
    )(h, mix, w)


N_EXTRACT = PEER_TOPK + 1
VAL_ROWS = 24
NOT_RANKED = 255.0


def _top_values(s, tm, want_rank=False):
    rowi = lax.broadcasted_iota(jnp.int32, (VAL_ROWS, tm), 0)
    vals = jnp.full((VAL_ROWS, tm), -jnp.inf, F32)
    rank = jnp.full(s.shape, NOT_RANKED, F32) if want_rank else None
    for i in range(N_EXTRACT):
        m = jnp.max(s, axis=0, keepdims=True)
        vals = jnp.where(rowi == i, m, vals)
        hit = s == m
        if want_rank:
            rank = jnp.where(hit, float(i), rank)
        s = jnp.where(hit, -jnp.inf, s)
    return (vals, rank) if want_rank else vals


def _select_chunk(s1, s2):
    row8 = lax.broadcasted_iota(jnp.int32, (SUBLANES, LANES), 0)
    row24 = lax.broadcasted_iota(jnp.int32, (VAL_ROWS, LANES), 0)
    v1 = _top_values(s1, LANES)
    v2, rank2 = _top_values(s2, LANES, want_rank=True)
    pieces = [jnp.where(row24 < N_EXTRACT, v1[0:1] + v2, -jnp.inf)]
    for i in range(1, SUBLANES):
        lim = N_EXTRACT // (i + 1)
        pieces.append(jnp.where(row8 < lim, v1[i:i + 1] + v2[0:SUBLANES], -jnp.inf))
    pieces.append(v1[SUBLANES:VAL_ROWS] + v2[0:1])
    cand = jnp.concatenate(pieces, axis=0)
    best = _top_values(cand, LANES)
    thr = 0.5 * (best[PEER_TOPK - 1:PEER_TOPK] + best[PEER_TOPK:PEER_TOPK + 1])
    zsum = jnp.sum(jnp.where(cand >= thr, jnp.exp(cand - best[0:1]), 0.0), axis=0, keepdims=True)
    c = thr - s1
    n = jnp.zeros_like(c)
    for j in range(PEER_TOPK):
        n = n + jnp.where(v2[j:j + 1] >= c, 1.0, 0.0)
    return rank2, jnp.exp(s2 - v2[0:1]) / zsum, n, jnp.exp(s1 - v1[0:1])


def _route_kernel(h_ref, nw_ref, wqt_ref, kkt_ref, xnt_ref, r2_ref, p2_ref, n_ref, p1_ref, st_ref,
                  *, tm):
    xn = _rms(h_ref[...], nw_ref[...])
    xnb = xn.astype(BF16)
    xnt_ref[...] = xn.T.astype(BF16)
    qt = lax.dot_general(wqt_ref[...], xnb, NT_DIMS, preferred_element_type=F32)
    st_ref[...] = jnp.dot(kkt_ref[...], qt.astype(BF16), preferred_element_type=F32)
    def per_head(h, carry):
        r1 = pl.multiple_of(h * (2 * PEER_NKEYS), 2 * PEER_NKEYS)
        r0 = pl.multiple_of(h * PEER_NKEYS, PEER_NKEYS)
        for tc in range(tm // LANES):
            cols = slice(tc * LANES, (tc + 1) * LANES)
            s1 = st_ref[pl.ds(r1, PEER_NKEYS), cols]
            s2 = st_ref[pl.ds(r1 + PEER_NKEYS, PEER_NKEYS), cols]
            rank2, p2, n, p1 = _select_chunk(s1, s2)
            r2_ref[pl.ds(r0, PEER_NKEYS), cols] = rank2.astype(BF16)
            p2_ref[pl.ds(r0, PEER_NKEYS), cols] = p2.astype(BF16)
            n_ref[pl.ds(r0, PEER_NKEYS), cols] = n
            p1_ref[pl.ds(r0, PEER_NKEYS), cols] = p1
        return carry

    lax.fori_loop(0, PEER_HEADS, per_head, 0, unroll=2)


def _route(h, nw, wqt, kkt, tm):
    t_pad = h.shape[0]
    hk = PEER_HEADS * PEER_NKEYS
    col = lambda i: (0, i)
    fixed = lambda i: (0, 0)
    return pl.pallas_call(
        functools.partial(_route_kernel, tm=tm),
        grid=(t_pad // tm,),
        in_specs=[
            pl.BlockSpec((tm, D_MODEL), lambda i: (i, 0)),
            pl.BlockSpec((1, D_MODEL), fixed),
            pl.BlockSpec((PEER_HEADS * PEER_QDIM, D_MODEL), fixed),
            pl.BlockSpec((2 * hk, PEER_HEADS * PEER_QDIM), fixed),
        ],
        out_specs=[pl.BlockSpec((D_MODEL, tm), col)] + [pl.BlockSpec((hk, tm), col)] * 4,
        out_shape=[jax.ShapeDtypeStruct((D_MODEL, t_pad), BF16)]
        + [jax.ShapeDtypeStruct((hk, t_pad), dt) for dt in (BF16, BF16, F32, F32)],
        scratch_shapes=[pltpu.VMEM((2 * hk, tm), F32)],
        compiler_params=pltpu.CompilerParams(
            dimension_semantics=("arbitrary",), vmem_limit_bytes=VMEM_LIMIT),
        name="peer_route",
    )(h, nw, wqt, kkt)


def _peer_kernel(xnt_ref, *refs, na, tm):
    u_refs = refs[:WEIGHT_PARTS]
    vt_refs = refs[WEIGHT_PARTS:2 * WEIGHT_PARTS]
    (r2_in_ref, p2_in_ref, n_ref, p1_ref, h_ref, o_ref,
     acc_ref, hid_ref, w_ref, r2_ref, p2_ref) = refs[2 * WEIGHT_PARTS:]
    te = na * PEER_NKEYS
    u_rows = te // WEIGHT_PARTS
    v_rows = D_MODEL // WEIGHT_PARTS
    e = pl.program_id(1)

    @pl.when(e == 0)
    def _():
        acc_ref[...] = jnp.zeros_like(acc_ref)
        r2_ref[...] = r2_in_ref[...]
        p2_ref[...] = p2_in_ref[...]

    for part, u_ref in enumerate(u_refs):
        hid_ref[part * u_rows:(part + 1) * u_rows, :] = jnp.dot(
            u_ref[...], xnt_ref[...], preferred_element_type=F32).astype(BF16)
    assert na == SUBLANES
    a0 = pl.multiple_of(e * na, SUBLANES)
    for tc in range(tm // LANES):
        cols = slice(tc * LANES, (tc + 1) * LANES)
        for al0 in range(0, na, GATE_GROUP):
            gates = [None] * GATE_GROUP
            for h in range(PEER_HEADS):
                rows = slice(h * PEER_NKEYS, (h + 1) * PEER_NKEYS)
                r2 = r2_ref[rows, cols]
                p2 = p2_ref[rows, cols]
                n_blk = n_ref[pl.ds(h * PEER_NKEYS + a0, na), cols]
                p_blk = p1_ref[pl.ds(h * PEER_NKEYS + a0, na), cols]
                for k in range(GATE_GROUP):
                    al = al0 + k
                    n_row = n_blk[al:al + 1].astype(BF16)
                    p_row = p_blk[al:al + 1].astype(BF16)
                    term = jnp.where(r2 < n_row, p2, 0) * p_row
                    gates[k] = term if gates[k] is None else gates[k] + term
            for k in range(GATE_GROUP):
                erows = slice((al0 + k) * PEER_NKEYS, (al0 + k + 1) * PEER_NKEYS)
                w_ref[erows, cols] = _gelu_tanh(hid_ref[erows, cols]) * gates[k]
    for part, vt_ref in enumerate(vt_refs):
        acc_ref[part * v_rows:(part + 1) * v_rows, :] += jnp.dot(
            vt_ref[...], w_ref[...], preferred_element_type=F32)

    @pl.when(e == pl.num_programs(1) - 1)
    def _():
        o_ref[...] = h_ref[...] + acc_ref[...].T


def _peer(xnt, u, vt, r2, p2, n, p1, h, tm, na):
    t_pad = h.shape[0]
    hk = PEER_HEADS * PEER_NKEYS
    te = na * PEER_NKEYS
    tok = lambda i, e: (0, i)
    return pl.pallas_call(
        functools.partial(_peer_kernel, na=na, tm=tm),
        grid=(t_pad // tm, PEER_NKEYS // na),
        in_specs=[pl.BlockSpec((D_MODEL, tm), tok)]
        + [pl.BlockSpec((te // WEIGHT_PARTS, D_MODEL),
                        functools.partial(lambda i, e, part: (e * WEIGHT_PARTS + part, 0), part=part))
           for part in range(WEIGHT_PARTS)]
        + [pl.BlockSpec((D_MODEL // WEIGHT_PARTS, te),
                        functools.partial(lambda i, e, part: (part, e), part=part))
           for part in range(WEIGHT_PARTS)]
        + [
            pl.BlockSpec((hk, tm), tok),
            pl.BlockSpec((hk, tm), tok),
            pl.BlockSpec((hk, tm), tok),
            pl.BlockSpec((hk, tm), tok),
            pl.BlockSpec((tm, D_MODEL), lambda i, e: (i, 0)),
        ],
        out_specs=pl.BlockSpec((tm, D_MODEL), lambda i, e: (i, 0)),
        out_shape=jax.ShapeDtypeStruct((t_pad, D_MODEL), F32),
        scratch_shapes=[pltpu.VMEM((D_MODEL, tm), F32), pltpu.VMEM((te, tm), BF16),
                        pltpu.VMEM((te, tm), BF16), pltpu.VMEM((hk, tm), BF16),
                        pltpu.VMEM((hk, tm), BF16)],
        compiler_params=pltpu.CompilerParams(
            dimension_semantics=("arbitrary", "arbitrary"), vmem_limit_bytes=VMEM_LIMIT),
        name="peer_experts",
    )(xnt, *([u] * WEIGHT_PARTS), *([vt] * WEIGHT_PARTS), r2, p2, n, p1, h)


def _final_norm_kernel(h_ref, w_ref, o_ref):
    o_ref[...] = _rms(h_ref[...], w_ref[...])


def _final_norm(h, w, tm):
    t_pad = h.shape[0]
    return pl.pallas_call(
        _final_norm_kernel,
        grid=(t_pad // tm,),
        in_specs=[pl.BlockSpec((tm, D_MODEL), lambda i: (i, 0)),
                  pl.BlockSpec((1, D_MODEL), lambda i: (0, 0))],
        out_specs=pl.BlockSpec((tm, D_MODEL), lambda i: (i, 0)),
        out_shape=jax.ShapeDtypeStruct((t_pad, D_MODEL), F32),
        name="final_norm",
    )(h, w)


def _pad_lanes(x, width):
    return jnp.pad(x, ((0, 0), (0, width - x.shape[-1])))


def kernel(x_prompt, x_sample, state_gla, state_ssm, state_conv, meta_tokens, norm1_w, w_in, w_gk2, b_gk2, gla_norm_w, conv_w, conv_b, dt_bias, a_log, d_skip, ssd_norm_w, w_out, norm2_w, peer_wq, peer_k1, peer_k2, peer_u, peer_v, final_norm_w):
    bp, seq_p, _ = x_prompt.shape
    bs, seq_s, _ = x_sample.shape
    depth = w_in.shape[0]
    lp = seq_p + N_META
    tp, ts = bp * lp, bs * seq_s
    tm = TOKEN_TILE
    t_pad = -(-(tp + ts) // tm) * tm
    assert lp % PROMPT_ROWS == 0 and seq_s % SUBLANES == 0
    assert bs % SAMPLE_SEQS_PER_STEP == 0 and tp % (SAMPLE_SEQS_PER_STEP * seq_s) == 0
    assert SSD_CONV - 1 <= seq_s and t_pad % ROUTE_TILE == 0

    hp = jnp.concatenate([jnp.broadcast_to(meta_tokens[None], (bp, N_META, D_MODEL)), x_prompt], axis=1)
    h = jnp.concatenate([hp.reshape(tp, D_MODEL), x_sample.reshape(ts, D_MODEL),
                         jnp.zeros((t_pad - tp - ts, D_MODEL), F32)], axis=0)

    offs = [0, GLA_KW, 2 * GLA_KW, 2 * GLA_KW + GLA_VW, QKVG_W, QKVG_W + GLA_RANK,
            QKVG_W + GLA_RANK + SSD_INNER, QKVG_W + GLA_RANK + SSD_INNER + CONV_DIM]
    w_flr = w_in[:, :, offs[4]:offs[5]]
    w_zx = w_in[:, :, offs[5]:offs[7]]
    w_dt = w_in[:, :, offs[7]:offs[7] + SSD_HEADS]
    w_misc = jnp.concatenate(
        [w_dt, w_flr, jnp.zeros((depth, D_MODEL, MISC_W - SSD_HEADS - GLA_RANK), F32)], axis=-1)
    w_proj = jnp.concatenate([w_in[:, :, 0:QKVG_W], w_zx, w_misc], axis=-1).astype(BF16)
    w_gk = jnp.concatenate(
        [jnp.zeros((depth, SSD_HEADS, GLA_KW), F32), w_gk2,
         jnp.zeros((depth, MISC_W - SSD_HEADS - GLA_RANK, GLA_KW), F32)], axis=1).astype(BF16)
    w_out_b = w_out.astype(BF16)
    wq_t = jnp.swapaxes(peer_wq, 1, 2).astype(BF16)
    half = PEER_QDIM // 2
    zeros_k = jnp.zeros((depth, PEER_NKEYS, half), F32)
    pair = jnp.concatenate([jnp.concatenate([peer_k1, zeros_k], axis=2),
                            jnp.concatenate([zeros_k, peer_k2], axis=2)], axis=1)
    kk_t = jnp.einsum('hg,lab->lhagb', jnp.eye(PEER_HEADS, dtype=F32), pair).reshape(
        depth, PEER_HEADS * 2 * PEER_NKEYS, PEER_HEADS * PEER_QDIM).astype(BF16)
    u_b = peer_u.astype(BF16)
    v_t = jnp.swapaxes(peer_v, 1, 2).astype(BF16)
    dtb = _pad_lanes(dt_bias, MISC_W)
    alog = _pad_lanes(a_log, MISC_W)
    dsk = _pad_lanes(d_skip, MISC_W)

    zero_gla = jnp.zeros((bp, GLA_DV, GLA_KW), F32)
    zero_ssm = jnp.zeros((bp, SSD_HEADS, SSD_HEADDIM, SSD_DSTATE), F32)
    zero_conv = jnp.zeros((bp, SSD_CONV - 1, CONV_DIM), F32)
    pad_rows = t_pad - tp - ts

    def gla_state_in(s):
        return jnp.transpose(s, (0, 3, 1, 2)).reshape(s.shape[0], GLA_DV, GLA_KW)

    def gla_state_out(s):
        return jnp.transpose(s.reshape(s.shape[0], GLA_DV, GLA_HEADS, GLA_DK), (0, 2, 3, 1))

    outs = [[] for _ in range(6)]
    for l in range(depth):
        qkvg, gk, zx, dtr = _in_proj(h, norm1_w[l][None], w_proj[l], w_gk[l], b_gk2[l][None], tm)
        mix_w = (gla_norm_w[l][None], conv_w[l], conv_b[l][None], dtb[l][None], alog[l][None],
                 dsk[l][None], ssd_norm_w[l][None])
        mix_p, sg_p, ss_p, sc_p = _mixers(
            qkvg, gk, zx, dtr, zero_gla, zero_ssm, zero_conv, *mix_w, row0=0, nseq=bp, nsq=1,
            nblk=lp // PROMPT_ROWS, nsub=PROMPT_ROWS // GLA_CHUNK, ch=GLA_CHUNK)
        mix_s, sg_s, ss_s, sc_s = _mixers(
            qkvg, gk, zx, dtr, gla_state_in(state_gla[l]), state_ssm[l], state_conv[l], *mix_w,
            row0=tp, nseq=bs, nsq=SAMPLE_SEQS_PER_STEP, nblk=1, nsub=1, ch=seq_s)
        mix = jnp.concatenate([mix_p, mix_s, jnp.zeros((pad_rows, MIX_W), F32)], axis=0)
        h = _out_proj(h, mix, w_out_b[l], tm)
        xnt, r2, p2, n, p1 = _route(h, norm2_w[l][None], wq_t[l], kk_t[l], ROUTE_TILE)
        h = _peer(xnt, u_b[l], v_t[l], r2, p2, n, p1, h, tm, PEER_A_PER_STEP)
        for lst, val in zip(outs, (gla_state_out(sg_p), ss_p, sc_p, gla_state_out(sg_s), ss_s, sc_s)):
            lst.append(val)

    y = _final_norm(h, final_norm_w[None], tm)
    y_prompt = y[:tp].reshape(bp, lp, D_MODEL)[:, N_META:]
    y_sample = y[tp:tp + ts].reshape(bs, seq_s, D_MODEL)
    return (y_prompt, y_sample) + tuple(jnp.stack(o) for o in outs)
```

```python
import functools
import math

import jax
import jax.numpy as jnp
from jax import lax
from jax.experimental import pallas as pl
from jax.experimental.pallas import tpu as pltpu

F32 = jnp.float32
BF16 = jnp.bfloat16

D_MODEL = 1024
N_META = 16
EPS = 1e-6
GLA_HEADS = 4
GLA_DK = 64
GLA_DV = 128
GLA_KW = GLA_HEADS * GLA_DK
GLA_VW = GLA_HEADS * GLA_DV
GLA_RANK = 16
GLA_TAU = 16.0
SSD_HEADS = 8
SSD_HEADDIM = 64
SSD_INNER = SSD_HEADS * SSD_HEADDIM
SSD_GROUPS = 2
SSD_REP = SSD_HEADS // SSD_GROUPS
SSD_DSTATE = 128
SSD_CONV = 4
CONV_DIM = SSD_INNER + 2 * SSD_GROUPS * SSD_DSTATE
PEER_HEADS = 8
PEER_NKEYS = 128
PEER_QDIM = 128
PEER_TOPK = 16

LANES = 128
SUBLANES = 8
MISC_W = LANES
QKVG_W = 2 * GLA_KW + 2 * GLA_VW
ZX_W = SSD_INNER + CONV_DIM
PROJ_W = QKVG_W + ZX_W + MISC_W
MIX_W = GLA_VW + SSD_INNER
PROMPT_ROWS = 48
SAMPLE_SEQS_PER_STEP = 4
GLA_CHUNK = 16
TOKEN_TILE = 512
ROUTE_TILE = 256
PEER_A_PER_STEP = 8
WEIGHT_PARTS = 4
GATE_GROUP = 2
VMEM_LIMIT = 56 * 1024 * 1024

NT_DIMS = (((1,), (1,)), ((), ()))
TN_DIMS = (((0,), (0,)), ((), ()))


def _rms(x, w):
    return x * lax.rsqrt(jnp.mean(x * x, axis=-1, keepdims=True) + EPS) * w


def _softplus(x):
    return jnp.maximum(x, 0.0) + jnp.log1p(jnp.exp(-jnp.abs(x)))


def _silu(x):
    return x * (1.0 / (1.0 + jnp.exp(-x)))


def _gelu_tanh(x):
    c = math.sqrt(2.0 / math.pi)
    return 0.5 * x * (1.0 + jnp.tanh(c * (x + 0.044715 * (x * x * x))))


def _chunk_cumsum(x, ch):
    row = lax.broadcasted_iota(jnp.int32, x.shape, 0) % ch
    sh = 1
    while sh < ch:
        x = x + jnp.where(row >= sh, pltpu.roll(x, sh, 0), 0.0)
        sh *= 2
    return x


def _in_proj_kernel(h_ref, nw_ref, w_ref, wgk_ref, bgk_ref, qkvg_ref, gk_ref, zx_ref, dt_ref):
    xn = _rms(h_ref[...], nw_ref[...]).astype(BF16)
    y = jnp.dot(xn, w_ref[...], preferred_element_type=F32)
    qkvg_ref[:, 0:GLA_KW] = y[:, 0:GLA_KW] * (GLA_DK ** -0.5)
    qkvg_ref[:, GLA_KW:QKVG_W] = y[:, GLA_KW:QKVG_W]
    zx_ref[...] = y[:, QKVG_W:QKVG_W + ZX_W]
    misc = y[:, QKVG_W + ZX_W:PROJ_W]
    dt_ref[...] = misc
    pre = jnp.dot(misc.astype(BF16), wgk_ref[...], preferred_element_type=F32) + bgk_ref[...]
    log_sig = jnp.minimum(pre, 0.0) - jnp.log1p(jnp.exp(-jnp.abs(pre)))
    gk_ref[...] = log_sig / GLA_TAU


def _in_proj(h, nw, w, wgk, bgk, tm):
    t_pad = h.shape[0]
    row = lambda i: (i, 0)
    fixed = lambda i: (0, 0)
    return pl.pallas_call(
        _in_proj_kernel,
        grid=(t_pad // tm,),
        in_specs=[
            pl.BlockSpec((tm, D_MODEL), row),
            pl.BlockSpec((1, D_MODEL), fixed),
            pl.BlockSpec((D_MODEL, PROJ_W), fixed),
            pl.BlockSpec((MISC_W, GLA_KW), fixed),
            pl.BlockSpec((1, GLA_KW), fixed),
        ],
        out_specs=[
            pl.BlockSpec((tm, QKVG_W), row),
            pl.BlockSpec((tm, GLA_KW), row),
            pl.BlockSpec((tm, ZX_W), row),
            pl.BlockSpec((tm, MISC_W), row),
        ],
        out_shape=[
            jax.ShapeDtypeStruct((t_pad, QKVG_W), F32),
            jax.ShapeDtypeStruct((t_pad, GLA_KW), F32),
            jax.ShapeDtypeStruct((t_pad, ZX_W), F32),
            jax.ShapeDtypeStruct((t_pad, MISC_W), F32),
        ],
        compiler_params=pltpu.CompilerParams(
            dimension_semantics=("arbitrary",), vmem_limit_bytes=VMEM_LIMIT),
        name="in_proj",
    )(h, nw, w, wgk, bgk)


def _gla_part(qkvg_ref, gk_ref, nw_ref, mix_ref, st_ref, *, nsq, nsub, ch):
    nchunk = nsq * nsub
    q = qkvg_ref[:, 0:GLA_KW]
    k = qkvg_ref[:, GLA_KW:2 * GLA_KW]
    v = qkvg_ref[:, 2 * GLA_KW:2 * GLA_KW + GLA_VW]
    g = qkvg_ref[:, 2 * GLA_KW + GLA_VW:QKVG_W]
    b = _chunk_cumsum(gk_ref[...], ch)

    b3 = b.reshape(nchunk, ch, GLA_KW)
    q3 = q.reshape(nchunk, ch, GLA_KW)
    k3 = k.reshape(nchunk, ch, GLA_KW)
    v3 = v.reshape(nchunk, ch, GLA_VW)
    tix = lax.broadcasted_iota(jnp.int32, (nchunk, ch, GLA_KW), 1)
    lane = lax.broadcasted_iota(jnp.int32, (nchunk, ch, LANES), 2)
    low_half = lane < GLA_DK
    acc = [jnp.zeros((nchunk, ch, GLA_DV), F32) for _ in range(GLA_HEADS)]
    for s in range(ch):
        diff = b3 - b3[:, s:s + 1, :]
        e = jnp.exp(jnp.where(tix >= s, diff, -jnp.inf))
        f = q3 * e * k3[:, s:s + 1, :]
        for h in range(GLA_HEADS):
            fh = f[:, :, (h // 2) * LANES:(h // 2 + 1) * LANES]
            mask = low_half if h % 2 == 0 else jnp.logical_not(low_half)
            col = jnp.sum(jnp.where(mask, fh, 0.0), axis=-1, keepdims=True)
            acc[h] = acc[h] + col * v3[:, s:s + 1, h * GLA_DV:(h + 1) * GLA_DV]

    nw = nw_ref[...]
    for i in range(nchunk):
        sq = i // nsub
        r0 = i * ch
        bi = b[r0:r0 + ch]
        bl = bi[ch - 1:ch]
        qh = (q[r0:r0 + ch] * jnp.exp(bi)).astype(BF16)
        kh = (k[r0:r0 + ch] * jnp.exp(bl - bi)).astype(BF16)
        vi = v[r0:r0 + ch].astype(BF16)
        st = st_ref[sq]
        stb = st.astype(BF16)
        kv = []
        for h in range(GLA_HEADS):
            ks = slice(h * GLA_DK, (h + 1) * GLA_DK)
            vs = slice(h * GLA_DV, (h + 1) * GLA_DV)
            o_inter = lax.dot_general(qh[:, ks], stb[:, ks], NT_DIMS, preferred_element_type=F32)
            kv.append(lax.dot_general(vi[:, vs], kh[:, ks], TN_DIMS, preferred_element_type=F32))
            o = acc[h][i] + o_inter
            gate = _silu(g[r0:r0 + ch, vs])
            mix_ref[r0:r0 + ch, vs] = _rms(o, nw) * gate
        st_ref[sq] = st * jnp.exp(bl) + jnp.concatenate(kv, axis=1)


def _ssd_part(zx_ref, dt_ref, cw_ref, cb_ref, dtb_ref, alog_ref, dsk_ref, nw_ref,
              mix_ref, hst_ref, cst_ref, carry_ref, *, sq, ch):
    rs = slice(sq * ch, (sq + 1) * ch)
    z = zx_ref[rs, 0:SSD_INNER]
    x = zx_ref[rs, SSD_INNER:ZX_W]
    full = jnp.concatenate([carry_ref[sq], x], axis=0)
    cw = cw_ref[...]
    base = SUBLANES - (SSD_CONV - 1)
    conv = cb_ref[...] + full[base:base + ch] * cw[0:1]
    for i in range(1, SSD_CONV):
        conv = conv + full[base + i:base + i + ch] * cw[i:i + 1]
    carry_ref[sq] = full[ch:ch + SUBLANES]
    cst_ref[sq] = full[ch + base:ch + SUBLANES]

    xc = _silu(conv)
    xs = xc[:, 0:SSD_INNER]
    bm = xc[:, SSD_INNER:SSD_INNER + SSD_GROUPS * SSD_DSTATE].astype(BF16)
    cm = xc[:, SSD_INNER + SSD_GROUPS * SSD_DSTATE:CONV_DIM].astype(BF16)

    dt = _softplus(dt_ref[rs, :] + dtb_ref[...])
    a = -jnp.exp(alog_ref[...])
    cum = _chunk_cumsum(dt * a, ch)
    eye = (lax.broadcasted_iota(jnp.int32, (SUBLANES, LANES), 0)
           == lax.broadcasted_iota(jnp.int32, (SUBLANES, LANES), 1)).astype(F32)
    cum_t = lax.dot_general(eye, cum, NT_DIMS, precision=lax.Precision.HIGHEST,
                            preferred_element_type=F32)
    dt_t = lax.dot_general(eye, dt, NT_DIMS, precision=lax.Precision.HIGHEST,
                           preferred_element_type=F32)
    causal = (lax.broadcasted_iota(jnp.int32, (ch, ch), 0)
              >= lax.broadcasted_iota(jnp.int32, (ch, ch), 1))
    dsk = dsk_ref[...]
    ys = []
    for grp in range(SSD_GROUPS):
        bg = bm[:, grp * SSD_DSTATE:(grp + 1) * SSD_DSTATE]
        cg = cm[:, grp * SSD_DSTATE:(grp + 1) * SSD_DSTATE]
        cb = lax.dot_general(cg, bg, NT_DIMS, preferred_element_type=F32)
        for rep in range(SSD_REP):
            hh = grp * SSD_REP + rep
            cum_c = cum[:, hh:hh + 1]
            lmat = jnp.exp(jnp.where(causal, cum_c - cum_t[hh:hh + 1, :], -jnp.inf))
            m = cb * lmat * dt_t[hh:hh + 1, :]
            xh = xs[:, hh * SSD_HEADDIM:(hh + 1) * SSD_HEADDIM]
            y_intra = jnp.dot(m.astype(BF16), xh.astype(BF16), preferred_element_type=F32)
            cl = cum[ch - 1:ch, hh:hh + 1]
            xw = xh * (jnp.exp(cl - cum_c) * dt[:, hh:hh + 1])
            hs = hst_ref[sq, hh]
            y_inter = lax.dot_general(cg, hs.astype(BF16), NT_DIMS, preferred_element_type=F32)
            hst_ref[sq, hh] = jnp.exp(cl) * hs + lax.dot_general(
                xw.astype(BF16), bg, TN_DIMS, preferred_element_type=F32)
            ys.append(y_intra + y_inter * jnp.exp(cum_c) + dsk[:, hh:hh + 1] * xh)
    y = jnp.concatenate(ys, axis=1) * _silu(z)
    mix_ref[rs, GLA_VW:MIX_W] = _rms(y, nw_ref[...])


def _mixer_kernel(qkvg_ref, gk_ref, zx_ref, dt_ref, s0_ref, h0_ref, c0_ref, gnw_ref,
                  cw_ref, cb_ref, dtb_ref, alog_ref, dsk_ref, snw_ref,
                  mix_ref, st_ref, hst_ref, cst_ref, carry_ref, *, nsq, nsub, ch):
    @pl.when(pl.program_id(1) == 0)
    def _():
        st_ref[...] = s0_ref[...]
        hst_ref[...] = h0_ref[...]
        carry_ref[...] = jnp.zeros_like(carry_ref)
        carry_ref[:, SUBLANES - (SSD_CONV - 1):SUBLANES, :] = c0_ref[...]

    _gla_part(qkvg_ref, gk_ref, gnw_ref, mix_ref, st_ref, nsq=nsq, nsub=nsub, ch=ch)
    for sq in range(nsq):
        _ssd_part(zx_ref, dt_ref, cw_ref, cb_ref, dtb_ref, alog_ref, dsk_ref, snw_ref,
                  mix_ref, hst_ref, cst_ref, carry_ref, sq=sq, ch=nsub * ch)


def _mixers(qkvg, gk, zx, dtr, s0t, h0, c0, gnw, cw, cb, dtb, alog, dsk, snw,
            *, row0, nseq, nsq, nblk, nsub, ch):
    assert nseq % nsq == 0 and (nsq == 1 or nblk == 1)
    rb = nsq * nsub * ch
    blk0 = row0 // rb
    row = lambda b, j: (blk0 + b * nblk + j, 0)
    fixed = lambda b, j: (0, 0)
    seq3 = lambda b, j: (b, 0, 0)
    seq4 = lambda b, j: (b, 0, 0, 0)
    return pl.pallas_call(
        functools.partial(_mixer_kernel, nsq=nsq, nsub=nsub, ch=ch),
        grid=(nseq // nsq, nblk),
        in_specs=[
            pl.BlockSpec((rb, QKVG_W), row),
            pl.BlockSpec((rb, GLA_KW), row),
            pl.BlockSpec((rb, ZX_W), row),
            pl.BlockSpec((rb, MISC_W), row),
            pl.BlockSpec((nsq, GLA_DV, GLA_KW), seq3),
            pl.BlockSpec((nsq, SSD_HEADS, SSD_HEADDIM, SSD_DSTATE), seq4),
            pl.BlockSpec((nsq, SSD_CONV - 1, CONV_DIM), seq3),
            pl.BlockSpec((1, GLA_DV), fixed),
            pl.BlockSpec((SSD_CONV, CONV_DIM), fixed),
            pl.BlockSpec((1, CONV_DIM), fixed),
            pl.BlockSpec((1, MISC_W), fixed),
            pl.BlockSpec((1, MISC_W), fixed),
            pl.BlockSpec((1, MISC_W), fixed),
            pl.BlockSpec((1, SSD_INNER), fixed),
        ],
        out_specs=[
            pl.BlockSpec((rb, MIX_W), lambda b, j: (b * nblk + j, 0)),
            pl.BlockSpec((nsq, GLA_DV, GLA_KW), seq3),
            pl.BlockSpec((nsq, SSD_HEADS, SSD_HEADDIM, SSD_DSTATE), seq4),
            pl.BlockSpec((nsq, SSD_CONV - 1, CONV_DIM), seq3),
        ],
        out_shape=[
            jax.ShapeDtypeStruct((nseq * nblk * nsub * ch, MIX_W), F32),
            jax.ShapeDtypeStruct((nseq, GLA_DV, GLA_KW), F32),
            jax.ShapeDtypeStruct((nseq, SSD_HEADS, SSD_HEADDIM, SSD_DSTATE), F32),
            jax.ShapeDtypeStruct((nseq, SSD_CONV - 1, CONV_DIM), F32),
        ],
        scratch_shapes=[pltpu.VMEM((nsq, SUBLANES, CONV_DIM), F32)],
        compiler_params=pltpu.CompilerParams(
            dimension_semantics=("arbitrary", "arbitrary"), vmem_limit_bytes=VMEM_LIMIT),
        name="token_mixers",
    )(qkvg, gk, zx, dtr, s0t, h0, c0, gnw, cw, cb, dtb, alog, dsk, snw)


def _out_proj_kernel(h_ref, mix_ref, w_ref, o_ref):
    o_ref[...] = h_ref[...] + jnp.dot(mix_ref[...].astype(BF16), w_ref[...],
                                      preferred_element_type=F32)


def _out_proj(h, mix, w, tm):
    t_pad = h.shape[0]
    row = lambda i: (i, 0)
    return pl.pallas_call(
        _out_proj_kernel,
        grid=(t_pad // tm,),
        in_specs=[
            pl.BlockSpec((tm, D_MODEL), row),
            pl.BlockSpec((tm, MIX_W), row),
            pl.BlockSpec((MIX_W, D_MODEL), lambda i: (0, 0)),
        ],
        out_specs=pl.BlockSpec((tm, D_MODEL), row),
        out_shape=jax.ShapeDtypeStruct((t_pad, D_MODEL), F32),
        compiler_params=pltpu.CompilerParams(
            dimension_semantics=("arbitrary",), vmem_limit_bytes=VMEM_LIMIT),
        name="out_proj",
    )(h, mix, w)


N_EXTRACT = PEER_TOPK + 1
VAL_ROWS = 24
NOT_RANKED = 255.0


def _top_values(s, tm, want_rank=False):
    rowi = lax.broadcasted_iota(jnp.int32, (VAL_ROWS, tm), 0)
    vals = jnp.full((VAL_ROWS, tm), -jnp.inf, F32)
    rank = jnp.full(s.shape, NOT_RANKED, F32) if want_rank else None
    for i in range(N_EXTRACT):
        m = jnp.max(s, axis=0, keepdims=True)
        vals = jnp.where(rowi == i, m, vals)
        hit = s == m
        if want_rank:
            rank = jnp.where(hit, float(i), rank)
        s = jnp.where(hit, -jnp.inf, s)
    return (vals, rank) if want_rank else vals


def _select_chunk(s1, s2):
    row8 = lax.broadcasted_iota(jnp.int32, (SUBLANES, LANES), 0)
    row24 = lax.broadcasted_iota(jnp.int32, (VAL_ROWS, LANES), 0)
    v1 = _top_values(s1, LANES)
    v2, rank2 = _top_values(s2, LANES, want_rank=True)
    pieces = [jnp.where(row24 < N_EXTRACT, v1[0:1] + v2, -jnp.inf)]
    for i in range(1, SUBLANES):
        lim = N_EXTRACT // (i + 1)
        pieces.append(jnp.where(row8 < lim, v1[i:i + 1] + v2[0:SUBLANES], -jnp.inf))
    pieces.append(v1[SUBLANES:VAL_ROWS] + v2[0:1])
    cand = jnp.concatenate(pieces, axis=0)
    best = _top_values(cand, LANES)
    thr = 0.5 * (best[PEER_TOPK - 1:PEER_TOPK] + best[PEER_TOPK:PEER_TOPK + 1])
    zsum = jnp.sum(jnp.where(cand >= thr, jnp.exp(cand - best[0:1]), 0.0), axis=0, keepdims=True)
    c = thr - s1
    n = jnp.zeros_like(c)
    for j in range(PEER_TOPK):
        n = n + jnp.where(v2[j:j + 1] >= c, 1.0, 0.0)
    return rank2, jnp.exp(s2 - v2[0:1]) / zsum, n, jnp.exp(s1 - v1[0:1])


def _route_kernel(h_ref, nw_ref, wqt_ref, kkt_ref, xnt_ref, r2_ref, p2_ref, n_ref, p1_ref, st_ref,
                  *, tm):
    xn = _rms(h_ref[...], nw_ref[...])
    xnb = xn.astype(BF16)
    xnt_ref[...] = xn.T.astype(BF16)
    qt = lax.dot_general(wqt_ref[...], xnb, NT_DIMS, preferred_element_type=F32)
    st_ref[...] = jnp.dot(kkt_ref[...], qt.astype(BF16), preferred_element_type=F32)
    def per_head(h, carry):
        r1 = pl.multiple_of(h * (2 * PEER_NKEYS), 2 * PEER_NKEYS)
        r0 = pl.multiple_of(h * PEER_NKEYS, PEER_NKEYS)
        for tc in range(tm // LANES):
            cols = slice(tc * LANES, (tc + 1) * LANES)
            s1 = st_ref[pl.ds(r1, PEER_NKEYS), cols]
            s2 = st_ref[pl.ds(r1 + PEER_NKEYS, PEER_NKEYS), cols]
            rank2, p2, n, p1 = _select_chunk(s1, s2)
            r2_ref[pl.ds(r0, PEER_NKEYS), cols] = rank2.astype(BF16)
            p2_ref[pl.ds(r0, PEER_NKEYS), cols] = p2.astype(BF16)
            n_ref[pl.ds(r0, PEER_NKEYS), cols] = n
            p1_ref[pl.ds(r0, PEER_NKEYS), cols] = p1
        return carry

    lax.fori_loop(0, PEER_HEADS, per_head, 0, unroll=2)


def _route(h, nw, wqt, kkt, tm):
    t_pad = h.shape[0]
    hk = PEER_HEADS * PEER_NKEYS
    col = lambda i: (0, i)
    fixed = lambda i: (0, 0)
    return pl.pallas_call(
        functools.partial(_route_kernel, tm=tm),
        grid=(t_pad // tm,),
        in_specs=[
            pl.BlockSpec((tm, D_MODEL), lambda i: (i, 0)),
            pl.BlockSpec((1, D_MODEL), fixed),
            pl.BlockSpec((PEER_HEADS * PEER_QDIM, D_MODEL), fixed),
            pl.BlockSpec((2 * hk, PEER_HEADS * PEER_QDIM), fixed),
        ],
        out_specs=[pl.BlockSpec((D_MODEL, tm), col)] + [pl.BlockSpec((hk, tm), col)] * 4,
        out_shape=[jax.ShapeDtypeStruct((D_MODEL, t_pad), BF16)]
        + [jax.ShapeDtypeStruct((hk, t_pad), dt) for dt in (BF16, BF16, F32, F32)],
        scratch_shapes=[pltpu.VMEM((2 * hk, tm), F32)],
        compiler_params=pltpu.CompilerParams(
            dimension_semantics=("arbitrary",), vmem_limit_bytes=VMEM_LIMIT),
        name="peer_route",
    )(h, nw, wqt, kkt)


def _peer_kernel(xnt_ref, *refs, na, tm):
    u_refs = refs[:WEIGHT_PARTS]
    vt_refs = refs[WEIGHT_PARTS:2 * WEIGHT_PARTS]
    (r2_in_ref, p2_in_ref, n_ref, p1_ref, h_ref, o_ref,
     acc_ref, hid_ref, w_ref, r2_ref, p2_ref) = refs[2 * WEIGHT_PARTS:]
    te = na * PEER_NKEYS
    u_rows = te // WEIGHT_PARTS
    v_rows = D_MODEL // WEIGHT_PARTS
    e = pl.program_id(1)

    @pl.when(e == 0)
    def _():
        acc_ref[...] = jnp.zeros_like(acc_ref)
        r2_ref[...] = r2_in_ref[...]
        p2_ref[...] = p2_in_ref[...]

    for part, u_ref in enumerate(u_refs):
        hid_ref[part * u_rows:(part + 1) * u_rows, :] = jnp.dot(
            u_ref[...], xnt_ref[...], preferred_element_type=F32).astype(BF16)
    _weighted_activations(hid_ref, w_ref, r2_ref, p2_ref, n_ref, p1_ref, e, na=na, tm=tm)
    for part, vt_ref in enumerate(vt_refs):
        acc_ref[part * v_rows:(part + 1) * v_rows, :] += jnp.dot(
            vt_ref[...], w_ref[...], preferred_element_type=F32)

    @pl.when(e == pl.num_programs(1) - 1)
    def _():
        o_ref[...] = h_ref[...] + acc_ref[...].T


def _weighted_activations(hid_ref, w_ref, r2_ref, p2_ref, n_ref, p1_ref, e, *, na, tm, live=None,
                          al_groups=None):
    assert na == SUBLANES
    a0 = pl.multiple_of(e * na, SUBLANES)
    if al_groups is None:
        al_groups = range(na // GATE_GROUP)
    for tc in range(tm // LANES):
        cols = slice(tc * LANES, (tc + 1) * LANES)
        for al0 in [g * GATE_GROUP for g in al_groups]:
            gates = [None] * GATE_GROUP
            for h in range(PEER_HEADS):
                rows = slice(h * PEER_NKEYS, (h + 1) * PEER_NKEYS)
                r2 = r2_ref[rows, cols]
                p2 = p2_ref[rows, cols]
                n_blk = n_ref[pl.ds(h * PEER_NKEYS + a0, na), cols]
                p_blk = p1_ref[pl.ds(h * PEER_NKEYS + a0, na), cols]
                if live is not None:
                    p_blk = jnp.where(live, p_blk, 0.0)
                for k in range(GATE_GROUP):
                    al = al0 + k
                    n_row = n_blk[al:al + 1].astype(BF16)
                    p_row = p_blk[al:al + 1].astype(BF16)
                    term = jnp.where(r2 < n_row, p2, 0) * p_row
                    gates[k] = term if gates[k] is None else gates[k] + term
            for k in range(GATE_GROUP):
                erows = slice((al0 + k) * PEER_NKEYS, (al0 + k + 1) * PEER_NKEYS)
                w_ref[erows, cols] = _gelu_tanh(hid_ref[erows, cols]) * gates[k]


def _peer(xnt, u, vt, r2, p2, n, p1, h, tm, na):
    t_pad = h.shape[0]
    hk = PEER_HEADS * PEER_NKEYS
    te = na * PEER_NKEYS
    tok = lambda i, e: (0, i)
    return pl.pallas_call(
        functools.partial(_peer_kernel, na=na, tm=tm),
        grid=(t_pad // tm, PEER_NKEYS // na),
        in_specs=[pl.BlockSpec((D_MODEL, tm), tok)]
        + [pl.BlockSpec((te // WEIGHT_PARTS, D_MODEL),
                        functools.partial(lambda i, e, part: (e * WEIGHT_PARTS + part, 0), part=part))
           for part in range(WEIGHT_PARTS)]
        + [pl.BlockSpec((D_MODEL // WEIGHT_PARTS, te),
                        functools.partial(lambda i, e, part: (part, e), part=part))
           for part in range(WEIGHT_PARTS)]
        + [
            pl.BlockSpec((hk, tm), tok),
            pl.BlockSpec((hk, tm), tok),
            pl.BlockSpec((hk, tm), tok),
            pl.BlockSpec((hk, tm), tok),
            pl.BlockSpec((tm, D_MODEL), lambda i, e: (i, 0)),
        ],
        out_specs=pl.BlockSpec((tm, D_MODEL), lambda i, e: (i, 0)),
        out_shape=jax.ShapeDtypeStruct((t_pad, D_MODEL), F32),
        scratch_shapes=[pltpu.VMEM((D_MODEL, tm), F32), pltpu.VMEM((te, tm), BF16),
                        pltpu.VMEM((te, tm), BF16), pltpu.VMEM((hk, tm), BF16),
                        pltpu.VMEM((hk, tm), BF16)],
        compiler_params=pltpu.CompilerParams(
            dimension_semantics=("arbitrary", "arbitrary"), vmem_limit_bytes=VMEM_LIMIT),
        name="peer_experts",
    )(xnt, *([u] * WEIGHT_PARTS), *([vt] * WEIGHT_PARTS), r2, p2, n, p1, h)


def _peer_pipe_kernel(xnt_ref, u_ref, vt_ref, r2_in_ref, p2_in_ref, n_ref, p1_ref, h_ref, o_ref,
                      acc_ref, hid0_ref, hid1_ref, w0_ref, w1_ref, r2_ref, p2_ref, *, na, tm, ne):
    s = pl.program_id(1)

    @pl.when(s == 0)
    def _():
        acc_ref[...] = jnp.zeros_like(acc_ref)
        r2_ref[...] = r2_in_ref[...]
        p2_ref[...] = p2_in_ref[...]

    @pl.when((s == 0) & (pl.program_id(0) == 0))
    def _():
        hid1_ref[...] = jnp.zeros_like(hid1_ref)
        w0_ref[...] = jnp.zeros_like(w0_ref)

    live = (s >= 1) & (s <= ne)
    e_gate = jnp.clip(s - 1, 0, ne - 1)

    def phase(hid_new, hid_old, w_new, w_old):
        n_slab = na // GATE_GROUP
        u_rows = na * PEER_NKEYS // n_slab
        v_rows = D_MODEL // n_slab
        for k in range(n_slab):
            ur = slice(k * u_rows, (k + 1) * u_rows)
            vr = slice(k * v_rows, (k + 1) * v_rows)
            hid_new[ur, :] = jnp.dot(u_ref[ur, :], xnt_ref[...],
                                     preferred_element_type=F32).astype(BF16)
            _weighted_activations(hid_old, w_new, r2_ref, p2_ref, n_ref, p1_ref, e_gate,
                                  na=na, tm=tm, live=live, al_groups=[k])
            acc_ref[vr, :] += jnp.dot(vt_ref[vr, :], w_old[...], preferred_element_type=F32)

    @pl.when(s % 2 == 0)
    def _():
        phase(hid0_ref, hid1_ref, w1_ref, w0_ref)

    @pl.when(s % 2 == 1)
    def _():
        phase(hid1_ref, hid0_ref, w0_ref, w1_ref)

    @pl.when(s == ne + 1)
    def _():
        o_ref[...] = h_ref[...] + acc_ref[...].T


def _peer_pipe(xnt, u, vt, r2, p2, n, p1, h, tm, na):
    t_pad = h.shape[0]
    hk = PEER_HEADS * PEER_NKEYS
    te = na * PEER_NKEYS
    ne = PEER_NKEYS // na
    tok = lambda i, s: (0, i)
    return pl.pallas_call(
        functools.partial(_peer_pipe_kernel, na=na, tm=tm, ne=ne),
        grid=(t_pad // tm, ne + 2),
        in_specs=[
            pl.BlockSpec((D_MODEL, tm), tok),
            pl.BlockSpec((te, D_MODEL), lambda i, s: (jnp.minimum(s, ne - 1), 0)),
            pl.BlockSpec((D_MODEL, te), lambda i, s: (0, jnp.clip(s - 2, 0, ne - 1))),
            pl.BlockSpec((hk, tm), tok),
            pl.BlockSpec((hk, tm), tok),
            pl.BlockSpec((hk, tm), tok),
            pl.BlockSpec((hk, tm), tok),
            pl.BlockSpec((tm, D_MODEL), lambda i, s: (i, 0)),
        ],
        out_specs=pl.BlockSpec((tm, D_MODEL), lambda i, s: (i, 0)),
        out_shape=jax.ShapeDtypeStruct((t_pad, D_MODEL), F32),
        scratch_shapes=[pltpu.VMEM((D_MODEL, tm), F32)]
        + [pltpu.VMEM((te, tm), BF16)] * 4 + [pltpu.VMEM((hk, tm), BF16)] * 2,
        compiler_params=pltpu.CompilerParams(
            dimension_semantics=("arbitrary", "arbitrary"), vmem_limit_bytes=VMEM_LIMIT),
        name="peer_experts_pipe",
    )(xnt, u, vt, r2, p2, n, p1, h)


def _final_norm_kernel(h_ref, w_ref, o_ref):
    o_ref[...] = _rms(h_ref[...], w_ref[...])


def _final_norm(h, w, tm):
    t_pad = h.shape[0]
    return pl.pallas_call(
        _final_norm_kernel,
        grid=(t_pad // tm,),
        in_specs=[pl.BlockSpec((tm, D_MODEL), lambda i: (i, 0)),
                  pl.BlockSpec((1, D_MODEL), lambda i: (0, 0))],
        out_specs=pl.BlockSpec((tm, D_MODEL), lambda i: (i, 0)),
        out_shape=jax.ShapeDtypeStruct((t_pad, D_MODEL), F32),
        name="final_norm",
    )(h, w)


def _pad_lanes(x, width):
    return jnp.pad(x, ((0, 0), (0, width - x.shape[-1])))


def kernel(x_prompt, x_sample, state_gla, state_ssm, state_conv, meta_tokens, norm1_w, w_in, w_gk2, b_gk2, gla_norm_w, conv_w, conv_b, dt_bias, a_log, d_skip, ssd_norm_w, w_out, norm2_w, peer_wq, peer_k1, peer_k2, peer_u, peer_v, final_norm_w):
    bp, seq_p, _ = x_prompt.shape
    bs, seq_s, _ = x_sample.shape
    depth = w_in.shape[0]
    lp = seq_p + N_META
    tp, ts = bp * lp, bs * seq_s
    tm = TOKEN_TILE
    t_pad = -(-(tp + ts) // tm) * tm
    assert lp % PROMPT_ROWS == 0 and seq_s % SUBLANES == 0
    assert bs % SAMPLE_SEQS_PER_STEP == 0 and tp % (SAMPLE_SEQS_PER_STEP * seq_s) == 0
    assert SSD_CONV - 1 <= seq_s and t_pad % ROUTE_TILE == 0

    hp = jnp.concatenate([jnp.broadcast_to(meta_tokens[None], (bp, N_META, D_MODEL)), x_prompt], axis=1)
    h = jnp.concatenate([hp.reshape(tp, D_MODEL), x_sample.reshape(ts, D_MODEL),
                         jnp.zeros((t_pad - tp - ts, D_MODEL), F32)], axis=0)

    offs = [0, GLA_KW, 2 * GLA_KW, 2 * GLA_KW + GLA_VW, QKVG_W, QKVG_W + GLA_RANK,
            QKVG_W + GLA_RANK + SSD_INNER, QKVG_W + GLA_RANK + SSD_INNER + CONV_DIM]
    w_flr = w_in[:, :, offs[4]:offs[5]]
    w_zx = w_in[:, :, offs[5]:offs[7]]
    w_dt = w_in[:, :, offs[7]:offs[7] + SSD_HEADS]
    w_misc = jnp.concatenate(
        [w_dt, w_flr, jnp.zeros((depth, D_MODEL, MISC_W - SSD_HEADS - GLA_RANK), F32)], axis=-1)
    w_proj = jnp.concatenate([w_in[:, :, 0:QKVG_W], w_zx, w_misc], axis=-1).astype(BF16)
    w_gk = jnp.concatenate(
        [jnp.zeros((depth, SSD_HEADS, GLA_KW), F32), w_gk2,
         jnp.zeros((depth, MISC_W - SSD_HEADS - GLA_RANK, GLA_KW), F32)], axis=1).astype(BF16)
    w_out_b = w_out.astype(BF16)
    wq_t = jnp.swapaxes(peer_wq, 1, 2).astype(BF16)
    half = PEER_QDIM // 2
    zeros_k = jnp.zeros((depth, PEER_NKEYS, half), F32)
    pair = jnp.concatenate([jnp.concatenate([peer_k1, zeros_k], axis=2),
                            jnp.concatenate([zeros_k, peer_k2], axis=2)], axis=1)
    kk_t = jnp.einsum('hg,lab->lhagb', jnp.eye(PEER_HEADS, dtype=F32), pair).reshape(
        depth, PEER_HEADS * 2 * PEER_NKEYS, PEER_HEADS * PEER_QDIM).astype(BF16)
    u_b = peer_u.astype(BF16)
    v_t = jnp.swapaxes(peer_v, 1, 2).astype(BF16)
    dtb = _pad_lanes(dt_bias, MISC_W)
    alog = _pad_lanes(a_log, MISC_W)
    dsk = _pad_lanes(d_skip, MISC_W)

    zero_gla = jnp.zeros((bp, GLA_DV, GLA_KW), F32)
    zero_ssm = jnp.zeros((bp, SSD_HEADS, SSD_HEADDIM, SSD_DSTATE), F32)
    zero_conv = jnp.zeros((bp, SSD_CONV - 1, CONV_DIM), F32)
    pad_rows = t_pad - tp - ts

    def gla_state_in(s):
        return jnp.transpose(s, (0, 3, 1, 2)).reshape(s.shape[0], GLA_DV, GLA_KW)

    def gla_state_out(s):
        return jnp.transpose(s.reshape(s.shape[0], GLA_DV, GLA_HEADS, GLA_DK), (0, 2, 3, 1))

    outs = [[] for _ in range(6)]
    for l in range(depth):
        qkvg, gk, zx, dtr = _in_proj(h, norm1_w[l][None], w_proj[l], w_gk[l], b_gk2[l][None], tm)
        mix_w = (gla_norm_w[l][None], conv_w[l], conv_b[l][None], dtb[l][None], alog[l][None],
                 dsk[l][None], ssd_norm_w[l][None])
        mix_p, sg_p, ss_p, sc_p = _mixers(
            qkvg, gk, zx, dtr, zero_gla, zero_ssm, zero_conv, *mix_w, row0=0, nseq=bp, nsq=1,
            nblk=lp // PROMPT_ROWS, nsub=PROMPT_ROWS // GLA_CHUNK, ch=GLA_CHUNK)
        mix_s, sg_s, ss_s, sc_s = _mixers(
            qkvg, gk, zx, dtr, gla_state_in(state_gla[l]), state_ssm[l], state_conv[l], *mix_w,
            row0=tp, nseq=bs, nsq=SAMPLE_SEQS_PER_STEP, nblk=1, nsub=1, ch=seq_s)
        mix = jnp.concatenate([mix_p, mix_s, jnp.zeros((pad_rows, MIX_W), F32)], axis=0)
        h = _out_proj(h, mix, w_out_b[l], tm)
        xnt, r2, p2, n, p1 = _route(h, norm2_w[l][None], wq_t[l], kk_t[l], ROUTE_TILE)
        if l == 0:
            h = _peer(xnt, u_b[l], v_t[l], r2, p2, n, p1, h, tm, PEER_A_PER_STEP)
        elif l == 1:
            h = _peer(xnt, u_b[l], v_t[l], r2, p2, n, p1, h, tm // 2, PEER_A_PER_STEP)
        else:
            h = _peer_pipe(xnt, u_b[l], v_t[l], r2, p2, n, p1, h, tm, PEER_A_PER_STEP)
        for lst, val in zip(outs, (gla_state_out(sg_p), ss_p, sc_p, gla_state_out(sg_s), ss_s, sc_s)):
            lst.append(val)

    y = _final_norm(h, final_norm_w[None], tm)
    y_prompt = y[:tp].reshape(bp, lp, D_MODEL)[:, N_META:]
    y_sample = y[tp:tp + ts].reshape(bs, seq_s, D_MODEL)
    return (y_prompt, y_sample) + tuple(jnp.stack(o) for o in outs)
```

```python
import functools
import math

import jax
import jax.numpy as jnp
from jax import lax
from jax.experimental import pallas as pl
from jax.experimental.pallas import tpu as pltpu

F32 = jnp.float32
BF16 = jnp.bfloat16

D_MODEL = 1024
N_META = 16
EPS = 1e-6
GLA_HEADS = 4
GLA_DK = 64
GLA_DV = 128
GLA_KW = GLA_HEADS * GLA_DK
GLA_VW = GLA_HEADS * GLA_DV
GLA_RANK = 16
GLA_TAU = 16.0
SSD_HEADS = 8
SSD_HEADDIM = 64
SSD_INNER = SSD_HEADS * SSD_HEADDIM
SSD_GROUPS = 2
SSD_REP = SSD_HEADS // SSD_GROUPS
SSD_DSTATE = 128
SSD_CONV = 4
CONV_DIM = SSD_INNER + 2 * SSD_GROUPS * SSD_DSTATE
PEER_HEADS = 8
PEER_NKEYS = 128
PEER_QDIM = 128
PEER_TOPK = 16

LANES = 128
SUBLANES = 8
MISC_W = LANES
QKVG_W = 2 * GLA_KW + 2 * GLA_VW
ZX_W = SSD_INNER + CONV_DIM
PROJ_W = QKVG_W + ZX_W + MISC_W
MIX_W = GLA_VW + SSD_INNER
PROMPT_ROWS = 48
SAMPLE_SEQS_PER_STEP = 4
GLA_CHUNK = 16
TOKEN_TILE = 512
ROUTE_TILE = 256
PEER_A_PER_STEP = 8
PEER_MODES = (("plain", 8), ("packed", 8), ("packed_x", 8), ("packed_x", 16))
GATE_GROUP = 2
VMEM_LIMIT = 56 * 1024 * 1024

NT_DIMS = (((1,), (1,)), ((), ()))
TN_DIMS = (((0,), (0,)), ((), ()))


def _rms(x, w):
    return x * lax.rsqrt(jnp.mean(x * x, axis=-1, keepdims=True) + EPS) * w


def _softplus(x):
    return jnp.maximum(x, 0.0) + jnp.log1p(jnp.exp(-jnp.abs(x)))


def _silu(x):
    return x * (1.0 / (1.0 + jnp.exp(-x)))


def _gelu_tanh(x):
    c = math.sqrt(2.0 / math.pi)
    return 0.5 * x * (1.0 + jnp.tanh(c * (x + 0.044715 * (x * x * x))))


def _chunk_cumsum(x, ch):
    row = lax.broadcasted_iota(jnp.int32, x.shape, 0) % ch
    sh = 1
    while sh < ch:
        x = x + jnp.where(row >= sh, pltpu.roll(x, sh, 0), 0.0)
        sh *= 2
    return x


def _in_proj_kernel(h_ref, nw_ref, w_ref, wgk_ref, bgk_ref, qkvg_ref, gk_ref, zx_ref, dt_ref):
    xn = _rms(h_ref[...], nw_ref[...]).astype(BF16)
    y = jnp.dot(xn, w_ref[...], preferred_element_type=F32)
    qkvg_ref[:, 0:GLA_KW] = y[:, 0:GLA_KW] * (GLA_DK ** -0.5)
    qkvg_ref[:, GLA_KW:QKVG_W] = y[:, GLA_KW:QKVG_W]
    zx_ref[...] = y[:, QKVG_W:QKVG_W + ZX_W]
    misc = y[:, QKVG_W + ZX_W:PROJ_W]
    dt_ref[...] = misc
    pre = jnp.dot(misc.astype(BF16), wgk_ref[...], preferred_element_type=F32) + bgk_ref[...]
    log_sig = jnp.minimum(pre, 0.0) - jnp.log1p(jnp.exp(-jnp.abs(pre)))
    gk_ref[...] = log_sig / GLA_TAU


def _in_proj(h, nw, w, wgk, bgk, tm):
    t_pad = h.shape[0]
    row = lambda i: (i, 0)
    fixed = lambda i: (0, 0)
    return pl.pallas_call(
        _in_proj_kernel,
        grid=(t_pad // tm,),
        in_specs=[
            pl.BlockSpec((tm, D_MODEL), row),
            pl.BlockSpec((1, D_MODEL), fixed),
            pl.BlockSpec((D_MODEL, PROJ_W), fixed),
            pl.BlockSpec((MISC_W, GLA_KW), fixed),
            pl.BlockSpec((1, GLA_KW), fixed),
        ],
        out_specs=[
            pl.BlockSpec((tm, QKVG_W), row),
            pl.BlockSpec((tm, GLA_KW), row),
            pl.BlockSpec((tm, ZX_W), row),
            pl.BlockSpec((tm, MISC_W), row),
        ],
        out_shape=[
            jax.ShapeDtypeStruct((t_pad, QKVG_W), F32),
            jax.ShapeDtypeStruct((t_pad, GLA_KW), F32),
            jax.ShapeDtypeStruct((t_pad, ZX_W), F32),
            jax.ShapeDtypeStruct((t_pad, MISC_W), F32),
        ],
        compiler_params=pltpu.CompilerParams(
            dimension_semantics=("arbitrary",), vmem_limit_bytes=VMEM_LIMIT),
        name="in_proj",
    )(h, nw, w, wgk, bgk)


def _gla_part(qkvg_ref, gk_ref, nw_ref, mix_ref, st_ref, *, nsq, nsub, ch):
    nchunk = nsq * nsub
    q = qkvg_ref[:, 0:GLA_KW]
    k = qkvg_ref[:, GLA_KW:2 * GLA_KW]
    v = qkvg_ref[:, 2 * GLA_KW:2 * GLA_KW + GLA_VW]
    g = qkvg_ref[:, 2 * GLA_KW + GLA_VW:QKVG_W]
    b = _chunk_cumsum(gk_ref[...], ch)

    b3 = b.reshape(nchunk, ch, GLA_KW)
    q3 = q.reshape(nchunk, ch, GLA_KW)
    k3 = k.reshape(nchunk, ch, GLA_KW)
    v3 = v.reshape(nchunk, ch, GLA_VW)
    tix = lax.broadcasted_iota(jnp.int32, (nchunk, ch, GLA_KW), 1)
    lane = lax.broadcasted_iota(jnp.int32, (nchunk, ch, LANES), 2)
    low_half = lane < GLA_DK
    acc = [jnp.zeros((nchunk, ch, GLA_DV), F32) for _ in range(GLA_HEADS)]
    for s in range(ch):
        diff = b3 - b3[:, s:s + 1, :]
        e = jnp.exp(jnp.where(tix >= s, diff, -jnp.inf))
        f = q3 * e * k3[:, s:s + 1, :]
        for h in range(GLA_HEADS):
            fh = f[:, :, (h // 2) * LANES:(h // 2 + 1) * LANES]
            mask = low_half if h % 2 == 0 else jnp.logical_not(low_half)
            col = jnp.sum(jnp.where(mask, fh, 0.0), axis=-1, keepdims=True)
            acc[h] = acc[h] + col * v3[:, s:s + 1, h * GLA_DV:(h + 1) * GLA_DV]

    nw = nw_ref[...]
    for i in range(nchunk):
        sq = i // nsub
        r0 = i * ch
        bi = b[r0:r0 + ch]
        bl = bi[ch - 1:ch]
        qh = (q[r0:r0 + ch] * jnp.exp(bi)).astype(BF16)
        kh = (k[r0:r0 + ch] * jnp.exp(bl - bi)).astype(BF16)
        vi = v[r0:r0 + ch].astype(BF16)
        st = st_ref[sq]
        stb = st.astype(BF16)
        kv = []
        for h in range(GLA_HEADS):
            ks = slice(h * GLA_DK, (h + 1) * GLA_DK)
            vs = slice(h * GLA_DV, (h + 1) * GLA_DV)
            o_inter = lax.dot_general(qh[:, ks], stb[:, ks], NT_DIMS, preferred_element_type=F32)
            kv.append(lax.dot_general(vi[:, vs], kh[:, ks], TN_DIMS, preferred_element_type=F32))
            o = acc[h][i] + o_inter
            gate = _silu(g[r0:r0 + ch, vs])
            mix_ref[r0:r0 + ch, vs] = _rms(o, nw) * gate
        st_ref[sq] = st * jnp.exp(bl) + jnp.concatenate(kv, axis=1)


def _ssd_part(zx_ref, dt_ref, cw_ref, cb_ref, dtb_ref, alog_ref, dsk_ref, nw_ref,
              mix_ref, hst_ref, cst_ref, carry_ref, *, sq, ch):
    rs = slice(sq * ch, (sq + 1) * ch)
    z = zx_ref[rs, 0:SSD_INNER]
    x = zx_ref[rs, SSD_INNER:ZX_W]
    full = jnp.concatenate([carry_ref[sq], x], axis=0)
    cw = cw_ref[...]
    base = SUBLANES - (SSD_CONV - 1)
    conv = cb_ref[...] + full[base:base + ch] * cw[0:1]
    for i in range(1, SSD_CONV):
        conv = conv + full[base + i:base + i + ch] * cw[i:i + 1]
    carry_ref[sq] = full[ch:ch + SUBLANES]
    cst_ref[sq] = full[ch + base:ch + SUBLANES]

    xc = _silu(conv)
    xs = xc[:, 0:SSD_INNER]
    bm = xc[:, SSD_INNER:SSD_INNER + SSD_GROUPS * SSD_DSTATE].astype(BF16)
    cm = xc[:, SSD_INNER + SSD_GROUPS * SSD_DSTATE:CONV_DIM].astype(BF16)

    dt = _softplus(dt_ref[rs, :] + dtb_ref[...])
    a = -jnp.exp(alog_ref[...])
    cum = _chunk_cumsum(dt * a, ch)
    eye = (lax.broadcasted_iota(jnp.int32, (SUBLANES, LANES), 0)
           == lax.broadcasted_iota(jnp.int32, (SUBLANES, LANES), 1)).astype(F32)
    cum_t = lax.dot_general(eye, cum, NT_DIMS, precision=lax.Precision.HIGHEST,
                            preferred_element_type=F32)
    dt_t = lax.dot_general(eye, dt, NT_DIMS, precision=lax.Precision.HIGHEST,
                           preferred_element_type=F32)
    causal = (lax.broadcasted_iota(jnp.int32, (ch, ch), 0)
              >= lax.broadcasted_iota(jnp.int32, (ch, ch), 1))
    dsk = dsk_ref[...]
    ys = []
    for grp in range(SSD_GROUPS):
        bg = bm[:, grp * SSD_DSTATE:(grp + 1) * SSD_DSTATE]
        cg = cm[:, grp * SSD_DSTATE:(grp + 1) * SSD_DSTATE]
        cb = lax.dot_general(cg, bg, NT_DIMS, preferred_element_type=F32)
        for rep in range(SSD_REP):
            hh = grp * SSD_REP + rep
            cum_c = cum[:, hh:hh + 1]
            lmat = jnp.exp(jnp.where(causal, cum_c - cum_t[hh:hh + 1, :], -jnp.inf))
            m = cb * lmat * dt_t[hh:hh + 1, :]
            xh = xs[:, hh * SSD_HEADDIM:(hh + 1) * SSD_HEADDIM]
            y_intra = jnp.dot(m.astype(BF16), xh.astype(BF16), preferred_element_type=F32)
            cl = cum[ch - 1:ch, hh:hh + 1]
            xw = xh * (jnp.exp(cl - cum_c) * dt[:, hh:hh + 1])
            hs = hst_ref[sq, hh]
            y_inter = lax.dot_general(cg, hs.astype(BF16), NT_DIMS, preferred_element_type=F32)
            hst_ref[sq, hh] = jnp.exp(cl) * hs + lax.dot_general(
                xw.astype(BF16), bg, TN_DIMS, preferred_element_type=F32)
            ys.append(y_intra + y_inter * jnp.exp(cum_c) + dsk[:, hh:hh + 1] * xh)
    y = jnp.concatenate(ys, axis=1) * _silu(z)
    mix_ref[rs, GLA_VW:MIX_W] = _rms(y, nw_ref[...])


def _mixer_kernel(qkvg_ref, gk_ref, zx_ref, dt_ref, s0_ref, h0_ref, c0_ref, gnw_ref,
                  cw_ref, cb_ref, dtb_ref, alog_ref, dsk_ref, snw_ref,
                  mix_ref, st_ref, hst_ref, cst_ref, carry_ref, *, nsq, nsub, ch):
    @pl.when(pl.program_id(1) == 0)
    def _():
        st_ref[...] = s0_ref[...]
        hst_ref[...] = h0_ref[...]
        carry_ref[...] = jnp.zeros_like(carry_ref)
        carry_ref[:, SUBLANES - (SSD_CONV - 1):SUBLANES, :] = c0_ref[...]

    _gla_part(qkvg_ref, gk_ref, gnw_ref, mix_ref, st_ref, nsq=nsq, nsub=nsub, ch=ch)
    for sq in range(nsq):
        _ssd_part(zx_ref, dt_ref, cw_ref, cb_ref, dtb_ref, alog_ref, dsk_ref, snw_ref,
                  mix_ref, hst_ref, cst_ref, carry_ref, sq=sq, ch=nsub * ch)


def _mixers(qkvg, gk, zx, dtr, s0t, h0, c0, gnw, cw, cb, dtb, alog, dsk, snw,
            *, row0, nseq, nsq, nblk, nsub, ch):
    assert nseq % nsq == 0 and (nsq == 1 or nblk == 1)
    rb = nsq * nsub * ch
    blk0 = row0 // rb
    row = lambda b, j: (blk0 + b * nblk + j, 0)
    fixed = lambda b, j: (0, 0)
    seq3 = lambda b, j: (b, 0, 0)
    seq4 = lambda b, j: (b, 0, 0, 0)
    return pl.pallas_call(
        functools.partial(_mixer_kernel, nsq=nsq, nsub=nsub, ch=ch),
        grid=(nseq // nsq, nblk),
        in_specs=[
            pl.BlockSpec((rb, QKVG_W), row),
            pl.BlockSpec((rb, GLA_KW), row),
            pl.BlockSpec((rb, ZX_W), row),
            pl.BlockSpec((rb, MISC_W), row),
            pl.BlockSpec((nsq, GLA_DV, GLA_KW), seq3),
            pl.BlockSpec((nsq, SSD_HEADS, SSD_HEADDIM, SSD_DSTATE), seq4),
            pl.BlockSpec((nsq, SSD_CONV - 1, CONV_DIM), seq3),
            pl.BlockSpec((1, GLA_DV), fixed),
            pl.BlockSpec((SSD_CONV, CONV_DIM), fixed),
            pl.BlockSpec((1, CONV_DIM), fixed),
            pl.BlockSpec((1, MISC_W), fixed),
            pl.BlockSpec((1, MISC_W), fixed),
            pl.BlockSpec((1, MISC_W), fixed),
            pl.BlockSpec((1, SSD_INNER), fixed),
        ],
        out_specs=[
            pl.BlockSpec((rb, MIX_W), lambda b, j: (b * nblk + j, 0)),
            pl.BlockSpec((nsq, GLA_DV, GLA_KW), seq3),
            pl.BlockSpec((nsq, SSD_HEADS, SSD_HEADDIM, SSD_DSTATE), seq4),
            pl.BlockSpec((nsq, SSD_CONV - 1, CONV_DIM), seq3),
        ],
        out_shape=[
            jax.ShapeDtypeStruct((nseq * nblk * nsub * ch, MIX_W), F32),
            jax.ShapeDtypeStruct((nseq, GLA_DV, GLA_KW), F32),
            jax.ShapeDtypeStruct((nseq, SSD_HEADS, SSD_HEADDIM, SSD_DSTATE), F32),
            jax.ShapeDtypeStruct((nseq, SSD_CONV - 1, CONV_DIM), F32),
        ],
        scratch_shapes=[pltpu.VMEM((nsq, SUBLANES, CONV_DIM), F32)],
        compiler_params=pltpu.CompilerParams(
            dimension_semantics=("arbitrary", "arbitrary"), vmem_limit_bytes=VMEM_LIMIT),
        name="token_mixers",
    )(qkvg, gk, zx, dtr, s0t, h0, c0, gnw, cw, cb, dtb, alog, dsk, snw)


def _out_proj_kernel(h_ref, mix_ref, w_ref, o_ref):
    o_ref[...] = h_ref[...] + jnp.dot(mix_ref[...].astype(BF16), w_ref[...],
                                      preferred_element_type=F32)


def _out_proj(h, mix, w, tm):
    t_pad = h.shape[0]
    row = lambda i: (i, 0)
    return pl.pallas_call(
        _out_proj_kernel,
        grid=(t_pad // tm,),
        in_specs=[
            pl.BlockSpec((tm, D_MODEL), row),
            pl.BlockSpec((tm, MIX_W), row),
            pl.BlockSpec((MIX_W, D_MODEL), lambda i: (0, 0)),
        ],
        out_specs=pl.BlockSpec((tm, D_MODEL), row),
        out_shape=jax.ShapeDtypeStruct((t_pad, D_MODEL), F32),
        compiler_params=pltpu.CompilerParams(
            dimension_semantics=("arbitrary",), vmem_limit_bytes=VMEM_LIMIT),
        name="out_proj",
    )(h, mix, w)


N_EXTRACT = PEER_TOPK + 1
VAL_ROWS = 24
NOT_RANKED = 255.0


def _top_values(s, tm, want_rank=False):
    rowi = lax.broadcasted_iota(jnp.int32, (VAL_ROWS, tm), 0)
    vals = jnp.full((VAL_ROWS, tm), -jnp.inf, F32)
    rank = jnp.full(s.shape, NOT_RANKED, F32) if want_rank else None
    for i in range(N_EXTRACT):
        m = jnp.max(s, axis=0, keepdims=True)
        vals = jnp.where(rowi == i, m, vals)
        hit = s == m
        if want_rank:
            rank = jnp.where(hit, float(i), rank)
        s = jnp.where(hit, -jnp.inf, s)
    return (vals, rank) if want_rank else vals


def _select_chunk(s1, s2):
    row8 = lax.broadcasted_iota(jnp.int32, (SUBLANES, LANES), 0)
    row24 = lax.broadcasted_iota(jnp.int32, (VAL_ROWS, LANES), 0)
    v1 = _top_values(s1, LANES)
    v2, rank2 = _top_values(s2, LANES, want_rank=True)
    pieces = [jnp.where(row24 < N_EXTRACT, v1[0:1] + v2, -jnp.inf)]
    for i in range(1, SUBLANES):
        lim = N_EXTRACT // (i + 1)
        pieces.append(jnp.where(row8 < lim, v1[i:i + 1] + v2[0:SUBLANES], -jnp.inf))
    pieces.append(v1[SUBLANES:VAL_ROWS] + v2[0:1])
    cand = jnp.concatenate(pieces, axis=0)
    best = _top_values(cand, LANES)
    thr = 0.5 * (best[PEER_TOPK - 1:PEER_TOPK] + best[PEER_TOPK:PEER_TOPK + 1])
    zsum = jnp.sum(jnp.where(cand >= thr, jnp.exp(cand - best[0:1]), 0.0), axis=0, keepdims=True)
    c = thr - s1
    n = jnp.zeros_like(c)
    for j in range(PEER_TOPK):
        n = n + jnp.where(v2[j:j + 1] >= c, 1.0, 0.0)
    return rank2, jnp.exp(s2 - v2[0:1]) / zsum, n, jnp.exp(s1 - v1[0:1])


def _route_kernel(h_ref, nw_ref, wqt_ref, kkt_ref, xnt_ref, r2_ref, p2_ref, n_ref, p1_ref, st_ref,
                  *, tm, pack_x):
    xn = _rms(h_ref[...], nw_ref[...])
    xnb = xn.astype(BF16)
    xnt = xn.T.astype(BF16)
    xnt_ref[...] = pltpu.bitcast(xnt, jnp.uint32) if pack_x else xnt
    qt = lax.dot_general(wqt_ref[...], xnb, NT_DIMS, preferred_element_type=F32)
    st_ref[...] = jnp.dot(kkt_ref[...], qt.astype(BF16), preferred_element_type=F32)
    def per_head(h, carry):
        r1 = pl.multiple_of(h * (2 * PEER_NKEYS), 2 * PEER_NKEYS)
        r0 = pl.multiple_of(h * PEER_NKEYS, PEER_NKEYS)
        for tc in range(tm // LANES):
            cols = slice(tc * LANES, (tc + 1) * LANES)
            s1 = st_ref[pl.ds(r1, PEER_NKEYS), cols]
            s2 = st_ref[pl.ds(r1 + PEER_NKEYS, PEER_NKEYS), cols]
            rank2, p2, n, p1 = _select_chunk(s1, s2)
            r2_ref[pl.ds(r0, PEER_NKEYS), cols] = rank2.astype(BF16)
            p2_ref[pl.ds(r0, PEER_NKEYS), cols] = p2.astype(BF16)
            n_ref[pl.ds(r0, PEER_NKEYS), cols] = n
            p1_ref[pl.ds(r0, PEER_NKEYS), cols] = p1
        return carry

    lax.fori_loop(0, PEER_HEADS, per_head, 0, unroll=2)


def _route(h, nw, wqt, kkt, tm, pack_x):
    t_pad = h.shape[0]
    hk = PEER_HEADS * PEER_NKEYS
    col = lambda i: (0, i)
    fixed = lambda i: (0, 0)
    return pl.pallas_call(
        functools.partial(_route_kernel, tm=tm, pack_x=pack_x),
        grid=(t_pad // tm,),
        in_specs=[
            pl.BlockSpec((tm, D_MODEL), lambda i: (i, 0)),
            pl.BlockSpec((1, D_MODEL), fixed),
            pl.BlockSpec((PEER_HEADS * PEER_QDIM, D_MODEL), fixed),
            pl.BlockSpec((2 * hk, PEER_HEADS * PEER_QDIM), fixed),
        ],
        out_specs=[pl.BlockSpec((D_MODEL // (2 if pack_x else 1), tm), col)]
        + [pl.BlockSpec((hk, tm), col)] * 4,
        out_shape=[jax.ShapeDtypeStruct((D_MODEL // 2, t_pad), jnp.uint32) if pack_x
                   else jax.ShapeDtypeStruct((D_MODEL, t_pad), BF16)]
        + [jax.ShapeDtypeStruct((hk, t_pad), dt) for dt in (BF16, BF16, F32, F32)],
        scratch_shapes=[pltpu.VMEM((2 * hk, tm), F32)],
        compiler_params=pltpu.CompilerParams(
            dimension_semantics=("arbitrary",), vmem_limit_bytes=VMEM_LIMIT),
        name="peer_route",
    )(h, nw, wqt, kkt)


def _peer_kernel(xnt_ref, u_ref, vt_ref, r2_in_ref, p2_in_ref, n_ref, p1_ref, h_ref, o_ref,
                 acc_ref, hid_ref, w_ref, r2_ref, p2_ref, *stage_refs, na, tm, mode):
    e = pl.program_id(1)

    @pl.when(e == 0)
    def _():
        acc_ref[...] = jnp.zeros_like(acc_ref)
        r2_ref[...] = r2_in_ref[...]
        p2_ref[...] = p2_in_ref[...]

    if mode in ("packed", "packed_x"):
        u = pltpu.bitcast(u_ref[...], BF16)
        vt = pltpu.bitcast(vt_ref[...], BF16)
    else:
        u = u_ref[...]
        vt = vt_ref[...]
    xnt = pltpu.bitcast(xnt_ref[...], BF16) if mode == "packed_x" else xnt_ref[...]
    hid_ref[...] = jnp.dot(u, xnt, preferred_element_type=F32).astype(BF16)
    _weighted_activations(hid_ref, w_ref, r2_ref, p2_ref, n_ref, p1_ref, e, na=na, tm=tm)
    acc_ref[...] += jnp.dot(vt, w_ref[...], preferred_element_type=F32)

    @pl.when(e == pl.num_programs(1) - 1)
    def _():
        o_ref[...] = h_ref[...] + acc_ref[...].T


def _weighted_activations(hid_ref, w_ref, r2_ref, p2_ref, n_ref, p1_ref, e, *, na, tm, live=None,
                          al_groups=None):
    assert na % SUBLANES == 0 and SUBLANES % GATE_GROUP == 0
    a0 = pl.multiple_of(e * na, SUBLANES)
    if al_groups is None:
        al_groups = range(na // GATE_GROUP)
    for tc in range(tm // LANES):
        cols = slice(tc * LANES, (tc + 1) * LANES)
        for al0 in [g * GATE_GROUP for g in al_groups]:
            gates = [None] * GATE_GROUP
            blk0 = al0 // SUBLANES * SUBLANES
            for h in range(PEER_HEADS):
                rows = slice(h * PEER_NKEYS, (h + 1) * PEER_NKEYS)
                r2 = r2_ref[rows, cols]
                p2 = p2_ref[rows, cols]
                n_blk = n_ref[pl.ds(h * PEER_NKEYS + a0 + blk0, SUBLANES), cols]
                p_blk = p1_ref[pl.ds(h * PEER_NKEYS + a0 + blk0, SUBLANES), cols]
                if live is not None:
                    p_blk = jnp.where(live, p_blk, 0.0)
                for k in range(GATE_GROUP):
                    al = al0 - blk0 + k
                    n_row = n_blk[al:al + 1].astype(BF16)
                    p_row = p_blk[al:al + 1].astype(BF16)
                    term = jnp.where(r2 < n_row, p2, 0) * p_row
                    gates[k] = term if gates[k] is None else gates[k] + term
            for k in range(GATE_GROUP):
                erows = slice((al0 + k) * PEER_NKEYS, (al0 + k + 1) * PEER_NKEYS)
                w_ref[erows, cols] = _gelu_tanh(hid_ref[erows, cols]) * gates[k]


def _peer(xnt, u, vt, r2, p2, n, p1, h, tm, na, mode):
    t_pad = h.shape[0]
    hk = PEER_HEADS * PEER_NKEYS
    te = na * PEER_NKEYS
    tok = lambda i, e: (0, i)
    wpack = 2 if mode in ("packed", "packed_x") else 1
    xpack = 2 if mode == "packed_x" else 1
    stage = []
    deep = {}
    return pl.pallas_call(
        functools.partial(_peer_kernel, na=na, tm=tm, mode=mode),
        grid=(t_pad // tm, PEER_NKEYS // na),
        in_specs=[
            pl.BlockSpec((D_MODEL // xpack, tm), tok),
            pl.BlockSpec((te // wpack, D_MODEL), lambda i, e: (e, 0), **deep),
            pl.BlockSpec((D_MODEL // wpack, te), lambda i, e: (0, e), **deep),
            pl.BlockSpec((hk, tm), tok),
            pl.BlockSpec((hk, tm), tok),
            pl.BlockSpec((hk, tm), tok),
            pl.BlockSpec((hk, tm), tok),
            pl.BlockSpec((tm, D_MODEL), lambda i, e: (i, 0)),
        ],
        out_specs=pl.BlockSpec((tm, D_MODEL), lambda i, e: (i, 0)),
        out_shape=jax.ShapeDtypeStruct((t_pad, D_MODEL), F32),
        scratch_shapes=[pltpu.VMEM((D_MODEL, tm), F32), pltpu.VMEM((te, tm), BF16),
                        pltpu.VMEM((te, tm), BF16), pltpu.VMEM((hk, tm), BF16),
                        pltpu.VMEM((hk, tm), BF16)] + stage,
        compiler_params=pltpu.CompilerParams(
            dimension_semantics=("arbitrary", "arbitrary"), vmem_limit_bytes=VMEM_LIMIT),
        name="peer_experts_%s_%d" % (mode, na),
    )(xnt, u, vt, r2, p2, n, p1, h)


def _peer_pipe_kernel(xnt_ref, u_ref, vt_ref, r2_in_ref, p2_in_ref, n_ref, p1_ref, h_ref, o_ref,
                      acc_ref, hid0_ref, hid1_ref, w0_ref, w1_ref, r2_ref, p2_ref, *, na, tm, ne):
    s = pl.program_id(1)

    @pl.when(s == 0)
    def _():
        acc_ref[...] = jnp.zeros_like(acc_ref)
        r2_ref[...] = r2_in_ref[...]
        p2_ref[...] = p2_in_ref[...]

    @pl.when((s == 0) & (pl.program_id(0) == 0))
    def _():
        hid1_ref[...] = jnp.zeros_like(hid1_ref)
        w0_ref[...] = jnp.zeros_like(w0_ref)

    live = (s >= 1) & (s <= ne)
    e_gate = jnp.clip(s - 1, 0, ne - 1)

    def phase(hid_new, hid_old, w_new, w_old):
        n_slab = na // GATE_GROUP
        u_rows = na * PEER_NKEYS // n_slab
        v_rows = D_MODEL // n_slab
        for k in range(n_slab):
            ur = slice(k * u_rows, (k + 1) * u_rows)
            vr = slice(k * v_rows, (k + 1) * v_rows)
            hid_new[ur, :] = jnp.dot(u_ref[ur, :], xnt_ref[...],
                                     preferred_element_type=F32).astype(BF16)
            _weighted_activations(hid_old, w_new, r2_ref, p2_ref, n_ref, p1_ref, e_gate,
                                  na=na, tm=tm, live=live, al_groups=[k])
            acc_ref[vr, :] += jnp.dot(vt_ref[vr, :], w_old[...], preferred_element_type=F32)

    @pl.when(s % 2 == 0)
    def _():
        phase(hid0_ref, hid1_ref, w1_ref, w0_ref)

    @pl.when(s % 2 == 1)
    def _():
        phase(hid1_ref, hid0_ref, w0_ref, w1_ref)

    @pl.when(s == ne + 1)
    def _():
        o_ref[...] = h_ref[...] + acc_ref[...].T


def _peer_pipe(xnt, u, vt, r2, p2, n, p1, h, tm, na):
    t_pad = h.shape[0]
    hk = PEER_HEADS * PEER_NKEYS
    te = na * PEER_NKEYS
    ne = PEER_NKEYS // na
    tok = lambda i, s: (0, i)
    return pl.pallas_call(
        functools.partial(_peer_pipe_kernel, na=na, tm=tm, ne=ne),
        grid=(t_pad // tm, ne + 2),
        in_specs=[
            pl.BlockSpec((D_MODEL, tm), tok),
            pl.BlockSpec((te, D_MODEL), lambda i, s: (jnp.minimum(s, ne - 1), 0)),
            pl.BlockSpec((D_MODEL, te), lambda i, s: (0, jnp.clip(s - 2, 0, ne - 1))),
            pl.BlockSpec((hk, tm), tok),
            pl.BlockSpec((hk, tm), tok),
            pl.BlockSpec((hk, tm), tok),
            pl.BlockSpec((hk, tm), tok),
            pl.BlockSpec((tm, D_MODEL), lambda i, s: (i, 0)),
        ],
        out_specs=pl.BlockSpec((tm, D_MODEL), lambda i, s: (i, 0)),
        out_shape=jax.ShapeDtypeStruct((t_pad, D_MODEL), F32),
        scratch_shapes=[pltpu.VMEM((D_MODEL, tm), F32)]
        + [pltpu.VMEM((te, tm), BF16)] * 4 + [pltpu.VMEM((hk, tm), BF16)] * 2,
        compiler_params=pltpu.CompilerParams(
            dimension_semantics=("arbitrary", "arbitrary"), vmem_limit_bytes=VMEM_LIMIT),
        name="peer_experts_pipe",
    )(xnt, u, vt, r2, p2, n, p1, h)


def _final_norm_kernel(h_ref, w_ref, o_ref):
    o_ref[...] = _rms(h_ref[...], w_ref[...])


def _final_norm(h, w, tm):
    t_pad = h.shape[0]
    return pl.pallas_call(
        _final_norm_kernel,
        grid=(t_pad // tm,),
        in_specs=[pl.BlockSpec((tm, D_MODEL), lambda i: (i, 0)),
                  pl.BlockSpec((1, D_MODEL), lambda i: (0, 0))],
        out_specs=pl.BlockSpec((tm, D_MODEL), lambda i: (i, 0)),
        out_shape=jax.ShapeDtypeStruct((t_pad, D_MODEL), F32),
        name="final_norm",
    )(h, w)


def _pad_lanes(x, width):
    return jnp.pad(x, ((0, 0), (0, width - x.shape[-1])))


def _pack_rows(x):
    rows, cols = x.shape
    pairs = lax.bitcast_convert_type(x.reshape(rows // 2, 2, cols), jnp.uint16).astype(jnp.uint32)
    return pairs[:, 0, :] | (pairs[:, 1, :] << 16)


def kernel(x_prompt, x_sample, state_gla, state_ssm, state_conv, meta_tokens, norm1_w, w_in, w_gk2, b_gk2, gla_norm_w, conv_w, conv_b, dt_bias, a_log, d_skip, ssd_norm_w, w_out, norm2_w, peer_wq, peer_k1, peer_k2, peer_u, peer_v, final_norm_w):
    bp, seq_p, _ = x_prompt.shape
    bs, seq_s, _ = x_sample.shape
    depth = w_in.shape[0]
    lp = seq_p + N_META
    tp, ts = bp * lp, bs * seq_s
    tm = TOKEN_TILE
    t_pad = -(-(tp + ts) // tm) * tm
    assert lp % PROMPT_ROWS == 0 and seq_s % SUBLANES == 0
    assert bs % SAMPLE_SEQS_PER_STEP == 0 and tp % (SAMPLE_SEQS_PER_STEP * seq_s) == 0
    assert SSD_CONV - 1 <= seq_s and t_pad % ROUTE_TILE == 0

    hp = jnp.concatenate([jnp.broadcast_to(meta_tokens[None], (bp, N_META, D_MODEL)), x_prompt], axis=1)
    h = jnp.concatenate([hp.reshape(tp, D_MODEL), x_sample.reshape(ts, D_MODEL),
                         jnp.zeros((t_pad - tp - ts, D_MODEL), F32)], axis=0)

    offs = [0, GLA_KW, 2 * GLA_KW, 2 * GLA_KW + GLA_VW, QKVG_W, QKVG_W + GLA_RANK,
            QKVG_W + GLA_RANK + SSD_INNER, QKVG_W + GLA_RANK + SSD_INNER + CONV_DIM]
    w_flr = w_in[:, :, offs[4]:offs[5]]
    w_zx = w_in[:, :, offs[5]:offs[7]]
    w_dt = w_in[:, :, offs[7]:offs[7] + SSD_HEADS]
    w_misc = jnp.concatenate(
        [w_dt, w_flr, jnp.zeros((depth, D_MODEL, MISC_W - SSD_HEADS - GLA_RANK), F32)], axis=-1)
    w_proj = jnp.concatenate([w_in[:, :, 0:QKVG_W], w_zx, w_misc], axis=-1).astype(BF16)
    w_gk = jnp.concatenate(
        [jnp.zeros((depth, SSD_HEADS, GLA_KW), F32), w_gk2,
         jnp.zeros((depth, MISC_W - SSD_HEADS - GLA_RANK, GLA_KW), F32)], axis=1).astype(BF16)
    w_out_b = w_out.astype(BF16)
    wq_t = jnp.swapaxes(peer_wq, 1, 2).astype(BF16)
    half = PEER_QDIM // 2
    zeros_k = jnp.zeros((depth, PEER_NKEYS, half), F32)
    pair = jnp.concatenate([jnp.concatenate([peer_k1, zeros_k], axis=2),
                            jnp.concatenate([zeros_k, peer_k2], axis=2)], axis=1)
    kk_t = jnp.einsum('hg,lab->lhagb', jnp.eye(PEER_HEADS, dtype=F32), pair).reshape(
        depth, PEER_HEADS * 2 * PEER_NKEYS, PEER_HEADS * PEER_QDIM).astype(BF16)
    u_b = peer_u.astype(BF16)
    v_t = jnp.swapaxes(peer_v, 1, 2).astype(BF16)
    dtb = _pad_lanes(dt_bias, MISC_W)
    alog = _pad_lanes(a_log, MISC_W)
    dsk = _pad_lanes(d_skip, MISC_W)

    zero_gla = jnp.zeros((bp, GLA_DV, GLA_KW), F32)
    zero_ssm = jnp.zeros((bp, SSD_HEADS, SSD_HEADDIM, SSD_DSTATE), F32)
    zero_conv = jnp.zeros((bp, SSD_CONV - 1, CONV_DIM), F32)
    pad_rows = t_pad - tp - ts

    def gla_state_in(s):
        return jnp.transpose(s, (0, 3, 1, 2)).reshape(s.shape[0], GLA_DV, GLA_KW)

    def gla_state_out(s):
        return jnp.transpose(s.reshape(s.shape[0], GLA_DV, GLA_HEADS, GLA_DK), (0, 2, 3, 1))

    outs = [[] for _ in range(6)]
    for l in range(depth):
        qkvg, gk, zx, dtr = _in_proj(h, norm1_w[l][None], w_proj[l], w_gk[l], b_gk2[l][None], tm)
        mix_w = (gla_norm_w[l][None], conv_w[l], conv_b[l][None], dtb[l][None], alog[l][None],
                 dsk[l][None], ssd_norm_w[l][None])
        mix_p, sg_p, ss_p, sc_p = _mixers(
            qkvg, gk, zx, dtr, zero_gla, zero_ssm, zero_conv, *mix_w, row0=0, nseq=bp, nsq=1,
            nblk=lp // PROMPT_ROWS, nsub=PROMPT_ROWS // GLA_CHUNK, ch=GLA_CHUNK)
        mix_s, sg_s, ss_s, sc_s = _mixers(
            qkvg, gk, zx, dtr, gla_state_in(state_gla[l]), state_ssm[l], state_conv[l], *mix_w,
            row0=tp, nseq=bs, nsq=SAMPLE_SEQS_PER_STEP, nblk=1, nsub=1, ch=seq_s)
        mix = jnp.concatenate([mix_p, mix_s, jnp.zeros((pad_rows, MIX_W), F32)], axis=0)
        h = _out_proj(h, mix, w_out_b[l], tm)
        mode, na = PEER_MODES[l % len(PEER_MODES)]
        xnt, r2, p2, n, p1 = _route(h, norm2_w[l][None], wq_t[l], kk_t[l], ROUTE_TILE,
                                    mode == "packed_x")
        packed = mode in ("packed", "packed_x")
        h = _peer(xnt, _pack_rows(u_b[l]) if packed else u_b[l],
                  _pack_rows(v_t[l]) if packed else v_t[l], r2, p2, n, p1, h, tm, na, mode)
        for lst, val in zip(outs, (gla_state_out(sg_p), ss_p, sc_p, gla_state_out(sg_s), ss_s, sc_s)):
            lst.append(val)

    y = _final_norm(h, final_norm_w[None], tm)
    y_prompt = y[:tp].reshape(bp, lp, D_MODEL)[:, N_META:]
    y_sample = y[tp:tp + ts].reshape(bs, seq_s, D_MODEL)
    return (y_prompt, y_sample) + tuple(jnp.stack(o) for o in outs)
```

```python
import functools
import math

import jax
import jax.numpy as jnp
from jax import lax
from jax.experimental import pallas as pl
from jax.experimental.pallas import tpu as pltpu

F32 = jnp.float32
BF16 = jnp.bfloat16

D_MODEL = 1024
N_META = 16
EPS = 1e-6
GLA_HEADS = 4
GLA_DK = 64
GLA_DV = 128
GLA_KW = GLA_HEADS * GLA_DK
GLA_VW = GLA_HEADS * GLA_DV
GLA_RANK = 16
GLA_TAU = 16.0
SSD_HEADS = 8
SSD_HEADDIM = 64
SSD_INNER = SSD_HEADS * SSD_HEADDIM
SSD_GROUPS = 2
SSD_REP = SSD_HEADS // SSD_GROUPS
SSD_DSTATE = 128
SSD_CONV = 4
CONV_DIM = SSD_INNER + 2 * SSD_GROUPS * SSD_DSTATE
PEER_HEADS = 8
PEER_NKEYS = 128
PEER_QDIM = 128
PEER_TOPK = 16

LANES = 128
SUBLANES = 8
MISC_W = LANES
QKVG_W = 2 * GLA_KW + 2 * GLA_VW
ZX_W = SSD_INNER + CONV_DIM
PROJ_W = QKVG_W + ZX_W + MISC_W
MIX_W = GLA_VW + SSD_INNER
PROMPT_ROWS = 48
SAMPLE_SEQS_PER_STEP = 4
GLA_CHUNK = 16
TOKEN_TILE = 512
ROUTE_TILE = 256
PEER_A_PER_STEP = 16
GATE_GROUP = 2
VMEM_LIMIT = 56 * 1024 * 1024

NT_DIMS = (((1,), (1,)), ((), ()))
TN_DIMS = (((0,), (0,)), ((), ()))


def _rms(x, w):
    return x * lax.rsqrt(jnp.mean(x * x, axis=-1, keepdims=True) + EPS) * w


def _softplus(x):
    return jnp.maximum(x, 0.0) + jnp.log1p(jnp.exp(-jnp.abs(x)))


def _silu(x):
    return x * (1.0 / (1.0 + jnp.exp(-x)))


def _gelu_tanh(x):
    c = math.sqrt(2.0 / math.pi)
    return 0.5 * x * (1.0 + jnp.tanh(c * (x + 0.044715 * (x * x * x))))


def _chunk_cumsum(x, ch):
    row = lax.broadcasted_iota(jnp.int32, x.shape, 0) % ch
    sh = 1
    while sh < ch:
        x = x + jnp.where(row >= sh, pltpu.roll(x, sh, 0), 0.0)
        sh *= 2
    return x


def _in_proj_kernel(h_ref, nw_ref, w_ref, wgk_ref, bgk_ref, qkvg_ref, gk_ref, zx_ref, dt_ref):
    xn = _rms(h_ref[...], nw_ref[...]).astype(BF16)
    y = jnp.dot(xn, w_ref[...], preferred_element_type=F32)
    qkvg_ref[:, 0:GLA_KW] = y[:, 0:GLA_KW] * (GLA_DK ** -0.5)
    qkvg_ref[:, GLA_KW:QKVG_W] = y[:, GLA_KW:QKVG_W]
    zx_ref[...] = y[:, QKVG_W:QKVG_W + ZX_W]
    misc = y[:, QKVG_W + ZX_W:PROJ_W]
    dt_ref[...] = misc
    pre = jnp.dot(misc.astype(BF16), wgk_ref[...], preferred_element_type=F32) + bgk_ref[...]
    log_sig = jnp.minimum(pre, 0.0) - jnp.log1p(jnp.exp(-jnp.abs(pre)))
    gk_ref[...] = log_sig / GLA_TAU


def _in_proj(h, nw, w, wgk, bgk, tm):
    t_pad = h.shape[0]
    row = lambda i: (i, 0)
    fixed = lambda i: (0, 0)
    return pl.pallas_call(
        _in_proj_kernel,
        grid=(t_pad // tm,),
        in_specs=[
            pl.BlockSpec((tm, D_MODEL), row),
            pl.BlockSpec((1, D_MODEL), fixed),
            pl.BlockSpec((D_MODEL, PROJ_W), fixed),
            pl.BlockSpec((MISC_W, GLA_KW), fixed),
            pl.BlockSpec((1, GLA_KW), fixed),
        ],
        out_specs=[
            pl.BlockSpec((tm, QKVG_W), row),
            pl.BlockSpec((tm, GLA_KW), row),
            pl.BlockSpec((tm, ZX_W), row),
            pl.BlockSpec((tm, MISC_W), row),
        ],
        out_shape=[
            jax.ShapeDtypeStruct((t_pad, QKVG_W), F32),
            jax.ShapeDtypeStruct((t_pad, GLA_KW), F32),
            jax.ShapeDtypeStruct((t_pad, ZX_W), F32),
            jax.ShapeDtypeStruct((t_pad, MISC_W), F32),
        ],
        compiler_params=pltpu.CompilerParams(
            dimension_semantics=("arbitrary",), vmem_limit_bytes=VMEM_LIMIT),
        name="in_proj",
    )(h, nw, w, wgk, bgk)


def _gla_part(qkvg_ref, gk_ref, nw_ref, mix_ref, st_ref, *, nsq, nsub, ch):
    nchunk = nsq * nsub
    q = qkvg_ref[:, 0:GLA_KW]
    k = qkvg_ref[:, GLA_KW:2 * GLA_KW]
    v = qkvg_ref[:, 2 * GLA_KW:2 * GLA_KW + GLA_VW]
    g = qkvg_ref[:, 2 * GLA_KW + GLA_VW:QKVG_W]
    b = _chunk_cumsum(gk_ref[...], ch)

    b3 = b.reshape(nchunk, ch, GLA_KW)
    q3 = q.reshape(nchunk, ch, GLA_KW)
    k3 = k.reshape(nchunk, ch, GLA_KW)
    v3 = v.reshape(nchunk, ch, GLA_VW)
    tix = lax.broadcasted_iota(jnp.int32, (nchunk, ch, GLA_KW), 1)
    lane = lax.broadcasted_iota(jnp.int32, (nchunk, ch, LANES), 2)
    low_half = lane < GLA_DK
    acc = [jnp.zeros((nchunk, ch, GLA_DV), F32) for _ in range(GLA_HEADS)]
    for s in range(ch):
        diff = b3 - b3[:, s:s + 1, :]
        e = jnp.exp(jnp.where(tix >= s, diff, -jnp.inf))
        f = q3 * e * k3[:, s:s + 1, :]
        for h in range(GLA_HEADS):
            fh = f[:, :, (h // 2) * LANES:(h // 2 + 1) * LANES]
            mask = low_half if h % 2 == 0 else jnp.logical_not(low_half)
            col = jnp.sum(jnp.where(mask, fh, 0.0), axis=-1, keepdims=True)
            acc[h] = acc[h] + col * v3[:, s:s + 1, h * GLA_DV:(h + 1) * GLA_DV]

    nw = nw_ref[...]
    for i in range(nchunk):
        sq = i // nsub
        r0 = i * ch
        bi = b[r0:r0 + ch]
        bl = bi[ch - 1:ch]
        qh = (q[r0:r0 + ch] * jnp.exp(bi)).astype(BF16)
        kh = (k[r0:r0 + ch] * jnp.exp(bl - bi)).astype(BF16)
        vi = v[r0:r0 + ch].astype(BF16)
        st = st_ref[sq]
        stb = st.astype(BF16)
        kv = []
        for h in range(GLA_HEADS):
            ks = slice(h * GLA_DK, (h + 1) * GLA_DK)
            vs = slice(h * GLA_DV, (h + 1) * GLA_DV)
            o_inter = lax.dot_general(qh[:, ks], stb[:, ks], NT_DIMS, preferred_element_type=F32)
            kv.append(lax.dot_general(vi[:, vs], kh[:, ks], TN_DIMS, preferred_element_type=F32))
            o = acc[h][i] + o_inter
            gate = _silu(g[r0:r0 + ch, vs])
            mix_ref[r0:r0 + ch, vs] = _rms(o, nw) * gate
        st_ref[sq] = st * jnp.exp(bl) + jnp.concatenate(kv, axis=1)


def _ssd_part(zx_ref, dt_ref, cw_ref, cb_ref, dtb_ref, alog_ref, dsk_ref, nw_ref,
              mix_ref, hst_ref, cst_ref, carry_ref, *, sq, ch):
    rs = slice(sq * ch, (sq + 1) * ch)
    z = zx_ref[rs, 0:SSD_INNER]
    x = zx_ref[rs, SSD_INNER:ZX_W]
    full = jnp.concatenate([carry_ref[sq], x], axis=0)
    cw = cw_ref[...]
    base = SUBLANES - (SSD_CONV - 1)
    conv = cb_ref[...] + full[base:base + ch] * cw[0:1]
    for i in range(1, SSD_CONV):
        conv = conv + full[base + i:base + i + ch] * cw[i:i + 1]
    carry_ref[sq] = full[ch:ch + SUBLANES]
    cst_ref[sq] = full[ch + base:ch + SUBLANES]

    xc = _silu(conv)
    xs = xc[:, 0:SSD_INNER]
    bm = xc[:, SSD_INNER:SSD_INNER + SSD_GROUPS * SSD_DSTATE].astype(BF16)
    cm = xc[:, SSD_INNER + SSD_GROUPS * SSD_DSTATE:CONV_DIM].astype(BF16)

    dt = _softplus(dt_ref[rs, :] + dtb_ref[...])
    a = -jnp.exp(alog_ref[...])
    cum = _chunk_cumsum(dt * a, ch)
    eye = (lax.broadcasted_iota(jnp.int32, (SUBLANES, LANES), 0)
           == lax.broadcasted_iota(jnp.int32, (SUBLANES, LANES), 1)).astype(F32)
    cum_t = lax.dot_general(eye, cum, NT_DIMS, precision=lax.Precision.HIGHEST,
                            preferred_element_type=F32)
    dt_t = lax.dot_general(eye, dt, NT_DIMS, precision=lax.Precision.HIGHEST,
                           preferred_element_type=F32)
    causal = (lax.broadcasted_iota(jnp.int32, (ch, ch), 0)
              >= lax.broadcasted_iota(jnp.int32, (ch, ch), 1))
    dsk = dsk_ref[...]
    ys = []
    for grp in range(SSD_GROUPS):
        bg = bm[:, grp * SSD_DSTATE:(grp + 1) * SSD_DSTATE]
        cg = cm[:, grp * SSD_DSTATE:(grp + 1) * SSD_DSTATE]
        cb = lax.dot_general(cg, bg, NT_DIMS, preferred_element_type=F32)
        for rep in range(SSD_REP):
            hh = grp * SSD_REP + rep
            cum_c = cum[:, hh:hh + 1]
            lmat = jnp.exp(jnp.where(causal, cum_c - cum_t[hh:hh + 1, :], -jnp.inf))
            m = cb * lmat * dt_t[hh:hh + 1, :]
            xh = xs[:, hh * SSD_HEADDIM:(hh + 1) * SSD_HEADDIM]
            y_intra = jnp.dot(m.astype(BF16), xh.astype(BF16), preferred_element_type=F32)
            cl = cum[ch - 1:ch, hh:hh + 1]
            xw = xh * (jnp.exp(cl - cum_c) * dt[:, hh:hh + 1])
            hs = hst_ref[sq, hh]
            y_inter = lax.dot_general(cg, hs.astype(BF16), NT_DIMS, preferred_element_type=F32)
            hst_ref[sq, hh] = jnp.exp(cl) * hs + lax.dot_general(
                xw.astype(BF16), bg, TN_DIMS, preferred_element_type=F32)
            ys.append(y_intra + y_inter * jnp.exp(cum_c) + dsk[:, hh:hh + 1] * xh)
    y = jnp.concatenate(ys, axis=1) * _silu(z)
    mix_ref[rs, GLA_VW:MIX_W] = _rms(y, nw_ref[...])


def _mixer_kernel(qkvg_ref, gk_ref, zx_ref, dt_ref, s0_ref, h0_ref, c0_ref, gnw_ref,
                  cw_ref, cb_ref, dtb_ref, alog_ref, dsk_ref, snw_ref,
                  mix_ref, st_ref, hst_ref, cst_ref, carry_ref, *, nsq, nsub, ch):
    @pl.when(pl.program_id(1) == 0)
    def _():
        st_ref[...] = s0_ref[...]
        hst_ref[...] = h0_ref[...]
        carry_ref[...] = jnp.zeros_like(carry_ref)
        carry_ref[:, SUBLANES - (SSD_CONV - 1):SUBLANES, :] = c0_ref[...]

    _gla_part(qkvg_ref, gk_ref, gnw_ref, mix_ref, st_ref, nsq=nsq, nsub=nsub, ch=ch)
    for sq in range(nsq):
        _ssd_part(zx_ref, dt_ref, cw_ref, cb_ref, dtb_ref, alog_ref, dsk_ref, snw_ref,
                  mix_ref, hst_ref, cst_ref, carry_ref, sq=sq, ch=nsub * ch)


def _mixers(qkvg, gk, zx, dtr, s0t, h0, c0, gnw, cw, cb, dtb, alog, dsk, snw,
            *, row0, nseq, nsq, nblk, nsub, ch):
    assert nseq % nsq == 0 and (nsq == 1 or nblk == 1)
    rb = nsq * nsub * ch
    blk0 = row0 // rb
    row = lambda b, j: (blk0 + b * nblk + j, 0)
    fixed = lambda b, j: (0, 0)
    seq3 = lambda b, j: (b, 0, 0)
    seq4 = lambda b, j: (b, 0, 0, 0)
    return pl.pallas_call(
        functools.partial(_mixer_kernel, nsq=nsq, nsub=nsub, ch=ch),
        grid=(nseq // nsq, nblk),
        in_specs=[
            pl.BlockSpec((rb, QKVG_W), row),
            pl.BlockSpec((rb, GLA_KW), row),
            pl.BlockSpec((rb, ZX_W), row),
            pl.BlockSpec((rb, MISC_W), row),
            pl.BlockSpec((nsq, GLA_DV, GLA_KW), seq3),
            pl.BlockSpec((nsq, SSD_HEADS, SSD_HEADDIM, SSD_DSTATE), seq4),
            pl.BlockSpec((nsq, SSD_CONV - 1, CONV_DIM), seq3),
            pl.BlockSpec((1, GLA_DV), fixed),
            pl.BlockSpec((SSD_CONV, CONV_DIM), fixed),
            pl.BlockSpec((1, CONV_DIM), fixed),
            pl.BlockSpec((1, MISC_W), fixed),
            pl.BlockSpec((1, MISC_W), fixed),
            pl.BlockSpec((1, MISC_W), fixed),
            pl.BlockSpec((1, SSD_INNER), fixed),
        ],
        out_specs=[
            pl.BlockSpec((rb, MIX_W), lambda b, j: (b * nblk + j, 0)),
            pl.BlockSpec((nsq, GLA_DV, GLA_KW), seq3),
            pl.BlockSpec((nsq, SSD_HEADS, SSD_HEADDIM, SSD_DSTATE), seq4),
            pl.BlockSpec((nsq, SSD_CONV - 1, CONV_DIM), seq3),
        ],
        out_shape=[
            jax.ShapeDtypeStruct((nseq * nblk * nsub * ch, MIX_W), F32),
            jax.ShapeDtypeStruct((nseq, GLA_DV, GLA_KW), F32),
            jax.ShapeDtypeStruct((nseq, SSD_HEADS, SSD_HEADDIM, SSD_DSTATE), F32),
            jax.ShapeDtypeStruct((nseq, SSD_CONV - 1, CONV_DIM), F32),
        ],
        scratch_shapes=[pltpu.VMEM((nsq, SUBLANES, CONV_DIM), F32)],
        compiler_params=pltpu.CompilerParams(
            dimension_semantics=("arbitrary", "arbitrary"), vmem_limit_bytes=VMEM_LIMIT),
        name="token_mixers",
    )(qkvg, gk, zx, dtr, s0t, h0, c0, gnw, cw, cb, dtb, alog, dsk, snw)


def _out_proj_kernel(h_ref, mix_ref, w_ref, o_ref):
    o_ref[...] = h_ref[...] + jnp.dot(mix_ref[...].astype(BF16), w_ref[...],
                                      preferred_element_type=F32)


def _out_proj(h, mix, w, tm):
    t_pad = h.shape[0]
    row = lambda i: (i, 0)
    return pl.pallas_call(
        _out_proj_kernel,
        grid=(t_pad // tm,),
        in_specs=[
            pl.BlockSpec((tm, D_MODEL), row),
            pl.BlockSpec((tm, MIX_W), row),
            pl.BlockSpec((MIX_W, D_MODEL), lambda i: (0, 0)),
        ],
        out_specs=pl.BlockSpec((tm, D_MODEL), row),
        out_shape=jax.ShapeDtypeStruct((t_pad, D_MODEL), F32),
        compiler_params=pltpu.CompilerParams(
            dimension_semantics=("arbitrary",), vmem_limit_bytes=VMEM_LIMIT),
        name="out_proj",
    )(h, mix, w)


N_EXTRACT = PEER_TOPK + 1
VAL_ROWS = 24
NOT_RANKED = 255.0


def _top_values(s, tm, want_rank=False):
    rowi = lax.broadcasted_iota(jnp.int32, (VAL_ROWS, tm), 0)
    vals = jnp.full((VAL_ROWS, tm), -jnp.inf, F32)
    rank = jnp.full(s.shape, NOT_RANKED, F32) if want_rank else None
    for i in range(N_EXTRACT):
        m = jnp.max(s, axis=0, keepdims=True)
        vals = jnp.where(rowi == i, m, vals)
        hit = s == m
        if want_rank:
            rank = jnp.where(hit, float(i), rank)
        s = jnp.where(hit, -jnp.inf, s)
    return (vals, rank) if want_rank else vals


def _select_chunk(s1, s2):
    row8 = lax.broadcasted_iota(jnp.int32, (SUBLANES, LANES), 0)
    row24 = lax.broadcasted_iota(jnp.int32, (VAL_ROWS, LANES), 0)
    v1 = _top_values(s1, LANES)
    v2, rank2 = _top_values(s2, LANES, want_rank=True)
    pieces = [jnp.where(row24 < N_EXTRACT, v1[0:1] + v2, -jnp.inf)]
    for i in range(1, SUBLANES):
        lim = N_EXTRACT // (i + 1)
        pieces.append(jnp.where(row8 < lim, v1[i:i + 1] + v2[0:SUBLANES], -jnp.inf))
    pieces.append(v1[SUBLANES:VAL_ROWS] + v2[0:1])
    cand = jnp.concatenate(pieces, axis=0)
    best = _top_values(cand, LANES)
    thr = 0.5 * (best[PEER_TOPK - 1:PEER_TOPK] + best[PEER_TOPK:PEER_TOPK + 1])
    zsum = jnp.sum(jnp.where(cand >= thr, jnp.exp(cand - best[0:1]), 0.0), axis=0, keepdims=True)
    c = thr - s1
    n = jnp.zeros_like(c)
    for j in range(PEER_TOPK):
        n = n + jnp.where(v2[j:j + 1] >= c, 1.0, 0.0)
    return rank2, jnp.exp(s2 - v2[0:1]) / zsum, n, jnp.exp(s1 - v1[0:1])


def _route_kernel(h_ref, nw_ref, wqt_ref, kkt_ref, xnt_ref, r2_ref, p2_ref, n_ref, p1_ref, st_ref,
                  *, tm):
    xn = _rms(h_ref[...], nw_ref[...])
    xnb = xn.astype(BF16)
    xnt_ref[...] = xn.T.astype(BF16)
    qt = lax.dot_general(wqt_ref[...], xnb, NT_DIMS, preferred_element_type=F32)
    st_ref[...] = jnp.dot(kkt_ref[...], qt.astype(BF16), preferred_element_type=F32)
    def per_head(h, carry):
        r1 = pl.multiple_of(h * (2 * PEER_NKEYS), 2 * PEER_NKEYS)
        r0 = pl.multiple_of(h * PEER_NKEYS, PEER_NKEYS)
        for tc in range(tm // LANES):
            cols = slice(tc * LANES, (tc + 1) * LANES)
            s1 = st_ref[pl.ds(r1, PEER_NKEYS), cols]
            s2 = st_ref[pl.ds(r1 + PEER_NKEYS, PEER_NKEYS), cols]
            rank2, p2, n, p1 = _select_chunk(s1, s2)
            r2_ref[pl.ds(r0, PEER_NKEYS), cols] = rank2.astype(BF16)
            p2_ref[pl.ds(r0, PEER_NKEYS), cols] = p2.astype(BF16)
            n_ref[pl.ds(r0, PEER_NKEYS), cols] = n
            p1_ref[pl.ds(r0, PEER_NKEYS), cols] = p1
        return carry

    lax.fori_loop(0, PEER_HEADS, per_head, 0, unroll=2)


def _route(h, nw, wqt, kkt, tm):
    t_pad = h.shape[0]
    hk = PEER_HEADS * PEER_NKEYS
    col = lambda i: (0, i)
    fixed = lambda i: (0, 0)
    return pl.pallas_call(
        functools.partial(_route_kernel, tm=tm),
        grid=(t_pad // tm,),
        in_specs=[
            pl.BlockSpec((tm, D_MODEL), lambda i: (i, 0)),
            pl.BlockSpec((1, D_MODEL), fixed),
            pl.BlockSpec((PEER_HEADS * PEER_QDIM, D_MODEL), fixed),
            pl.BlockSpec((2 * hk, PEER_HEADS * PEER_QDIM), fixed),
        ],
        out_specs=[pl.BlockSpec((D_MODEL, tm), col)] + [pl.BlockSpec((hk, tm), col)] * 4,
        out_shape=[jax.ShapeDtypeStruct((D_MODEL, t_pad), BF16)]
        + [jax.ShapeDtypeStruct((hk, t_pad), dt) for dt in (BF16, BF16, F32, F32)],
        scratch_shapes=[pltpu.VMEM((2 * hk, tm), F32)],
        compiler_params=pltpu.CompilerParams(
            dimension_semantics=("arbitrary",), vmem_limit_bytes=VMEM_LIMIT),
        name="peer_route",
    )(h, nw, wqt, kkt)


def _peer_kernel(xnt_ref, u_ref, vt_ref, r2_in_ref, p2_in_ref, n_ref, p1_ref, h_ref, o_ref,
                 acc_ref, hid_ref, w_ref, r2_ref, p2_ref, *, na, tm):
    e = pl.program_id(1)

    @pl.when(e == 0)
    def _():
        acc_ref[...] = jnp.zeros_like(acc_ref)
        r2_ref[...] = r2_in_ref[...]
        p2_ref[...] = p2_in_ref[...]

    hid_ref[...] = jnp.dot(u_ref[...], xnt_ref[...], preferred_element_type=F32).astype(BF16)
    _weighted_activations(hid_ref, w_ref, r2_ref, p2_ref, n_ref, p1_ref, e, na=na, tm=tm)
    acc_ref[...] += jnp.dot(vt_ref[...], w_ref[...], preferred_element_type=F32)

    @pl.when(e == pl.num_programs(1) - 1)
    def _():
        o_ref[...] = h_ref[...] + acc_ref[...].T


def _weighted_activations(hid_ref, w_ref, r2_ref, p2_ref, n_ref, p1_ref, e, *, na, tm):
    assert na % SUBLANES == 0 and SUBLANES % GATE_GROUP == 0
    a0 = pl.multiple_of(e * na, SUBLANES)
    for tc in range(tm // LANES):
        cols = slice(tc * LANES, (tc + 1) * LANES)
        for al0 in range(0, na, GATE_GROUP):
            gates = [None] * GATE_GROUP
            blk0 = al0 // SUBLANES * SUBLANES
            for h in range(PEER_HEADS):
                rows = slice(h * PEER_NKEYS, (h + 1) * PEER_NKEYS)
                r2 = r2_ref[rows, cols]
                p2 = p2_ref[rows, cols]
                n_blk = n_ref[pl.ds(h * PEER_NKEYS + a0 + blk0, SUBLANES), cols]
                p_blk = p1_ref[pl.ds(h * PEER_NKEYS + a0 + blk0, SUBLANES), cols]
                for k in range(GATE_GROUP):
                    al = al0 - blk0 + k
                    n_row = n_blk[al:al + 1].astype(BF16)
                    p_row = p_blk[al:al + 1].astype(BF16)
                    term = jnp.where(r2 < n_row, p2, 0) * p_row
                    gates[k] = term if gates[k] is None else gates[k] + term
            for k in range(GATE_GROUP):
                erows = slice((al0 + k) * PEER_NKEYS, (al0 + k + 1) * PEER_NKEYS)
                w_ref[erows, cols] = _gelu_tanh(hid_ref[erows, cols]) * gates[k]


def _peer(xnt, u, vt, r2, p2, n, p1, h, tm, na):
    t_pad = h.shape[0]
    hk = PEER_HEADS * PEER_NKEYS
    te = na * PEER_NKEYS
    tok = lambda i, e: (0, i)
    return pl.pallas_call(
        functools.partial(_peer_kernel, na=na, tm=tm),
        grid=(t_pad // tm, PEER_NKEYS // na),
        in_specs=[
            pl.BlockSpec((D_MODEL, tm), tok),
            pl.BlockSpec((te, D_MODEL), lambda i, e: (e, 0)),
            pl.BlockSpec((D_MODEL, te), lambda i, e: (0, e)),
            pl.BlockSpec((hk, tm), tok),
            pl.BlockSpec((hk, tm), tok),
            pl.BlockSpec((hk, tm), tok),
            pl.BlockSpec((hk, tm), tok),
            pl.BlockSpec((tm, D_MODEL), lambda i, e: (i, 0)),
        ],
        out_specs=pl.BlockSpec((tm, D_MODEL), lambda i, e: (i, 0)),
        out_shape=jax.ShapeDtypeStruct((t_pad, D_MODEL), F32),
        scratch_shapes=[pltpu.VMEM((D_MODEL, tm), F32), pltpu.VMEM((te, tm), BF16),
                        pltpu.VMEM((te, tm), BF16), pltpu.VMEM((hk, tm), BF16),
                        pltpu.VMEM((hk, tm), BF16)],
        compiler_params=pltpu.CompilerParams(
            dimension_semantics=("arbitrary", "arbitrary"), vmem_limit_bytes=VMEM_LIMIT),
        name="peer_experts",
    )(xnt, u, vt, r2, p2, n, p1, h)


def _final_norm_kernel(h_ref, w_ref, o_ref):
    o_ref[...] = _rms(h_ref[...], w_ref[...])


def _final_norm(h, w, tm):
    t_pad = h.shape[0]
    return pl.pallas_call(
        _final_norm_kernel,
        grid=(t_pad // tm,),
        in_specs=[pl.BlockSpec((tm, D_MODEL), lambda i: (i, 0)),
                  pl.BlockSpec((1, D_MODEL), lambda i: (0, 0))],
        out_specs=pl.BlockSpec((tm, D_MODEL), lambda i: (i, 0)),
        out_shape=jax.ShapeDtypeStruct((t_pad, D_MODEL), F32),
        name="final_norm",
    )(h, w)


def _pad_lanes(x, width):
    return jnp.pad(x, ((0, 0), (0, width - x.shape[-1])))


def kernel(x_prompt, x_sample, state_gla, state_ssm, state_conv, meta_tokens, norm1_w, w_in, w_gk2, b_gk2, gla_norm_w, conv_w, conv_b, dt_bias, a_log, d_skip, ssd_norm_w, w_out, norm2_w, peer_wq, peer_k1, peer_k2, peer_u, peer_v, final_norm_w):
    bp, seq_p, _ = x_prompt.shape
    bs, seq_s, _ = x_sample.shape
    depth = w_in.shape[0]
    lp = seq_p + N_META
    tp, ts = bp * lp, bs * seq_s
    tm = TOKEN_TILE
    t_pad = -(-(tp + ts) // tm) * tm
    assert lp % PROMPT_ROWS == 0 and seq_s % SUBLANES == 0
    assert bs % SAMPLE_SEQS_PER_STEP == 0 and tp % (SAMPLE_SEQS_PER_STEP * seq_s) == 0
    assert SSD_CONV - 1 <= seq_s and t_pad % ROUTE_TILE == 0

    hp = jnp.concatenate([jnp.broadcast_to(meta_tokens[None], (bp, N_META, D_MODEL)), x_prompt], axis=1)
    h = jnp.concatenate([hp.reshape(tp, D_MODEL), x_sample.reshape(ts, D_MODEL),
                         jnp.zeros((t_pad - tp - ts, D_MODEL), F32)], axis=0)

    offs = [0, GLA_KW, 2 * GLA_KW, 2 * GLA_KW + GLA_VW, QKVG_W, QKVG_W + GLA_RANK,
            QKVG_W + GLA_RANK + SSD_INNER, QKVG_W + GLA_RANK + SSD_INNER + CONV_DIM]
    w_flr = w_in[:, :, offs[4]:offs[5]]
    w_zx = w_in[:, :, offs[5]:offs[7]]
    w_dt = w_in[:, :, offs[7]:offs[7] + SSD_HEADS]
    w_misc = jnp.concatenate(
        [w_dt, w_flr, jnp.zeros((depth, D_MODEL, MISC_W - SSD_HEADS - GLA_RANK), F32)], axis=-1)
    w_proj = jnp.concatenate([w_in[:, :, 0:QKVG_W], w_zx, w_misc], axis=-1).astype(BF16)
    w_gk = jnp.concatenate(
        [jnp.zeros((depth, SSD_HEADS, GLA_KW), F32), w_gk2,
         jnp.zeros((depth, MISC_W - SSD_HEADS - GLA_RANK, GLA_KW), F32)], axis=1).astype(BF16)
    w_out_b = w_out.astype(BF16)
    wq_t = jnp.swapaxes(peer_wq, 1, 2).astype(BF16)
    half = PEER_QDIM // 2
    zeros_k = jnp.zeros((depth, PEER_NKEYS, half), F32)
    pair = jnp.concatenate([jnp.concatenate([peer_k1, zeros_k], axis=2),
                            jnp.concatenate([zeros_k, peer_k2], axis=2)], axis=1)
    kk_t = jnp.einsum('hg,lab->lhagb', jnp.eye(PEER_HEADS, dtype=F32), pair).reshape(
        depth, PEER_HEADS * 2 * PEER_NKEYS, PEER_HEADS * PEER_QDIM).astype(BF16)
    u_b = peer_u.astype(BF16)
    v_t = jnp.swapaxes(peer_v, 1, 2).astype(BF16)
    dtb = _pad_lanes(dt_bias, MISC_W)
    alog = _pad_lanes(a_log, MISC_W)
    dsk = _pad_lanes(d_skip, MISC_W)

    zero_gla = jnp.zeros((bp, GLA_DV, GLA_KW), F32)
    zero_ssm = jnp.zeros((bp, SSD_HEADS, SSD_HEADDIM, SSD_DSTATE), F32)
    zero_conv = jnp.zeros((bp, SSD_CONV - 1, CONV_DIM), F32)
    pad_rows = t_pad - tp - ts

    def gla_state_in(s):
        return jnp.transpose(s, (0, 3, 1, 2)).reshape(s.shape[0], GLA_DV, GLA_KW)

    def gla_state_out(s):
        return jnp.transpose(s.reshape(s.shape[0], GLA_DV, GLA_HEADS, GLA_DK), (0, 2, 3, 1))

    outs = [[] for _ in range(6)]
    for l in range(depth):
        qkvg, gk, zx, dtr = _in_proj(h, norm1_w[l][None], w_proj[l], w_gk[l], b_gk2[l][None], tm)
        mix_w = (gla_norm_w[l][None], conv_w[l], conv_b[l][None], dtb[l][None], alog[l][None],
                 dsk[l][None], ssd_norm_w[l][None])
        mix_p, sg_p, ss_p, sc_p = _mixers(
            qkvg, gk, zx, dtr, zero_gla, zero_ssm, zero_conv, *mix_w, row0=0, nseq=bp, nsq=1,
            nblk=lp // PROMPT_ROWS, nsub=PROMPT_ROWS // GLA_CHUNK, ch=GLA_CHUNK)
        mix_s, sg_s, ss_s, sc_s = _mixers(
            qkvg, gk, zx, dtr, gla_state_in(state_gla[l]), state_ssm[l], state_conv[l], *mix_w,
            row0=tp, nseq=bs, nsq=SAMPLE_SEQS_PER_STEP, nblk=1, nsub=1, ch=seq_s)
        mix = jnp.concatenate([mix_p, mix_s, jnp.zeros((pad_rows, MIX_W), F32)], axis=0)
        h = _out_proj(h, mix, w_out_b[l], tm)
        xnt, r2, p2, n, p1 = _route(h, norm2_w[l][None], wq_t[l], kk_t[l], ROUTE_TILE)
        h = _peer(xnt, u_b[l], v_t[l], r2, p2, n, p1, h, tm, PEER_A_PER_STEP)
        for lst, val in zip(outs, (gla_state_out(sg_p), ss_p, sc_p, gla_state_out(sg_s), ss_s, sc_s)):
            lst.append(val)

    y = _final_norm(h, final_norm_w[None], tm)
    y_prompt = y[:tp].reshape(bp, lp, D_MODEL)[:, N_META:]
    y_sample = y[tp:tp + ts].reshape(bs, seq_s, D_MODEL)
    return (y_prompt, y_sample) + tuple(jnp.stack(o) for o in outs)
```

```python
import functools
import math

import jax
import jax.numpy as jnp
from jax import lax
from jax.experimental import pallas as pl
from jax.experimental.pallas import tpu as pltpu

F32 = jnp.float32
BF16 = jnp.bfloat16

D_MODEL = 1024
N_META = 16
EPS = 1e-6
GLA_HEADS = 4
GLA_DK = 64
GLA_DV = 128
GLA_KW = GLA_HEADS * GLA_DK
GLA_VW = GLA_HEADS * GLA_DV
GLA_RANK = 16
GLA_TAU = 16.0
SSD_HEADS = 8
SSD_HEADDIM = 64
SSD_INNER = SSD_HEADS * SSD_HEADDIM
SSD_GROUPS = 2
SSD_REP = SSD_HEADS // SSD_GROUPS
SSD_DSTATE = 128
SSD_CONV = 4
CONV_DIM = SSD_INNER + 2 * SSD_GROUPS * SSD_DSTATE
PEER_HEADS = 8
PEER_NKEYS = 128
PEER_QDIM = 128
PEER_TOPK = 16

LANES = 128
SUBLANES = 8
MISC_W = LANES
QKVG_W = 2 * GLA_KW + 2 * GLA_VW
ZX_W = SSD_INNER + CONV_DIM
PROJ_W = QKVG_W + ZX_W + MISC_W
MIX_W = GLA_VW + SSD_INNER
PROMPT_ROWS = 48
SAMPLE_SEQS_PER_STEP = 4
GLA_CHUNK = 16
TOKEN_TILE = 512
ROUTE_TILE = 512
PEER_A_PER_STEP = 16
GATE_GROUP = 2
VMEM_LIMIT = 56 * 1024 * 1024

NT_DIMS = (((1,), (1,)), ((), ()))
TN_DIMS = (((0,), (0,)), ((), ()))


def _rms(x, w):
    return x * lax.rsqrt(jnp.mean(x * x, axis=-1, keepdims=True) + EPS) * w


def _softplus(x):
    return jnp.maximum(x, 0.0) + jnp.log1p(jnp.exp(-jnp.abs(x)))


def _silu(x):
    return x * (1.0 / (1.0 + jnp.exp(-x)))


def _gelu_tanh(x):
    c = math.sqrt(2.0 / math.pi)
    return 0.5 * x * (1.0 + jnp.tanh(c * (x + 0.044715 * (x * x * x))))


def _chunk_cumsum(x, ch):
    row = lax.broadcasted_iota(jnp.int32, x.shape, 0) % ch
    sh = 1
    while sh < ch:
        x = x + jnp.where(row >= sh, pltpu.roll(x, sh, 0), 0.0)
        sh *= 2
    return x


def _in_proj_kernel(h_ref, nw_ref, w_ref, wgk_ref, bgk_ref, qkvg_ref, gk_ref, zx_ref, dt_ref):
    xn = _rms(h_ref[...], nw_ref[...]).astype(BF16)
    y = jnp.dot(xn, w_ref[...], preferred_element_type=F32)
    qkvg_ref[:, 0:GLA_KW] = y[:, 0:GLA_KW] * (GLA_DK ** -0.5)
    qkvg_ref[:, GLA_KW:QKVG_W] = y[:, GLA_KW:QKVG_W]
    zx_ref[...] = y[:, QKVG_W:QKVG_W + ZX_W]
    misc = y[:, QKVG_W + ZX_W:PROJ_W]
    dt_ref[...] = misc
    pre = jnp.dot(misc.astype(BF16), wgk_ref[...], preferred_element_type=F32) + bgk_ref[...]
    log_sig = jnp.minimum(pre, 0.0) - jnp.log1p(jnp.exp(-jnp.abs(pre)))
    gk_ref[...] = log_sig / GLA_TAU


def _in_proj(h, nw, w, wgk, bgk, tm):
    t_pad = h.shape[0]
    row = lambda i: (i, 0)
    fixed = lambda i: (0, 0)
    return pl.pallas_call(
        _in_proj_kernel,
        grid=(t_pad // tm,),
        in_specs=[
            pl.BlockSpec((tm, D_MODEL), row),
            pl.BlockSpec((1, D_MODEL), fixed),
            pl.BlockSpec((D_MODEL, PROJ_W), fixed),
            pl.BlockSpec((MISC_W, GLA_KW), fixed),
            pl.BlockSpec((1, GLA_KW), fixed),
        ],
        out_specs=[
            pl.BlockSpec((tm, QKVG_W), row),
            pl.BlockSpec((tm, GLA_KW), row),
            pl.BlockSpec((tm, ZX_W), row),
            pl.BlockSpec((tm, MISC_W), row),
        ],
        out_shape=[
            jax.ShapeDtypeStruct((t_pad, QKVG_W), F32),
            jax.ShapeDtypeStruct((t_pad, GLA_KW), F32),
            jax.ShapeDtypeStruct((t_pad, ZX_W), F32),
            jax.ShapeDtypeStruct((t_pad, MISC_W), F32),
        ],
        compiler_params=pltpu.CompilerParams(
            dimension_semantics=("arbitrary",), vmem_limit_bytes=VMEM_LIMIT),
        name="in_proj",
    )(h, nw, w, wgk, bgk)


def _gla_part(qkvg_ref, gk_ref, nw_ref, mix_ref, st_ref, *, nsq, nsub, ch):
    nchunk = nsq * nsub
    q = qkvg_ref[:, 0:GLA_KW]
    k = qkvg_ref[:, GLA_KW:2 * GLA_KW]
    v = qkvg_ref[:, 2 * GLA_KW:2 * GLA_KW + GLA_VW]
    g = qkvg_ref[:, 2 * GLA_KW + GLA_VW:QKVG_W]
    b = _chunk_cumsum(gk_ref[...], ch)

    b3 = b.reshape(nchunk, ch, GLA_KW)
    q3 = q.reshape(nchunk, ch, GLA_KW)
    k3 = k.reshape(nchunk, ch, GLA_KW)
    v3 = v.reshape(nchunk, ch, GLA_VW)
    tix = lax.broadcasted_iota(jnp.int32, (nchunk, ch, GLA_KW), 1)
    lane = lax.broadcasted_iota(jnp.int32, (nchunk, ch, LANES), 2)
    low_half = lane < GLA_DK
    acc = [jnp.zeros((nchunk, ch, GLA_DV), F32) for _ in range(GLA_HEADS)]
    for s in range(ch):
        diff = b3 - b3[:, s:s + 1, :]
        e = jnp.exp(jnp.where(tix >= s, diff, -jnp.inf))
        f = q3 * e * k3[:, s:s + 1, :]
        for h in range(GLA_HEADS):
            fh = f[:, :, (h // 2) * LANES:(h // 2 + 1) * LANES]
            mask = low_half if h % 2 == 0 else jnp.logical_not(low_half)
            col = jnp.sum(jnp.where(mask, fh, 0.0), axis=-1, keepdims=True)
            acc[h] = acc[h] + col * v3[:, s:s + 1, h * GLA_DV:(h + 1) * GLA_DV]

    nw = nw_ref[...]
    for i in range(nchunk):
        sq = i // nsub
        r0 = i * ch
        bi = b[r0:r0 + ch]
        bl = bi[ch - 1:ch]
        qh = (q[r0:r0 + ch] * jnp.exp(bi)).astype(BF16)
        kh = (k[r0:r0 + ch] * jnp.exp(bl - bi)).astype(BF16)
        vi = v[r0:r0 + ch].astype(BF16)
        st = st_ref[sq]
        stb = st.astype(BF16)
        kv = []
        for h in range(GLA_HEADS):
            ks = slice(h * GLA_DK, (h + 1) * GLA_DK)
            vs = slice(h * GLA_DV, (h + 1) * GLA_DV)
            o_inter = lax.dot_general(qh[:, ks], stb[:, ks], NT_DIMS, preferred_element_type=F32)
            kv.append(lax.dot_general(vi[:, vs], kh[:, ks], TN_DIMS, preferred_element_type=F32))
            o = acc[h][i] + o_inter
            gate = _silu(g[r0:r0 + ch, vs])
            mix_ref[r0:r0 + ch, vs] = _rms(o, nw) * gate
        st_ref[sq] = st * jnp.exp(bl) + jnp.concatenate(kv, axis=1)


def _ssd_part(zx_ref, dt_ref, cw_ref, cb_ref, dtb_ref, alog_ref, dsk_ref, nw_ref,
              mix_ref, hst_ref, cst_ref, carry_ref, *, sq, ch):
    rs = slice(sq * ch, (sq + 1) * ch)
    z = zx_ref[rs, 0:SSD_INNER]
    x = zx_ref[rs, SSD_INNER:ZX_W]
    full = jnp.concatenate([carry_ref[sq], x], axis=0)
    cw = cw_ref[...]
    base = SUBLANES - (SSD_CONV - 1)
    conv = cb_ref[...] + full[base:base + ch] * cw[0:1]
    for i in range(1, SSD_CONV):
        conv = conv + full[base + i:base + i + ch] * cw[i:i + 1]
    carry_ref[sq] = full[ch:ch + SUBLANES]
    cst_ref[sq] = full[ch + base:ch + SUBLANES]

    xc = _silu(conv)
    xs = xc[:, 0:SSD_INNER]
    bm = xc[:, SSD_INNER:SSD_INNER + SSD_GROUPS * SSD_DSTATE].astype(BF16)
    cm = xc[:, SSD_INNER + SSD_GROUPS * SSD_DSTATE:CONV_DIM].astype(BF16)

    dt = _softplus(dt_ref[rs, :] + dtb_ref[...])
    a = -jnp.exp(alog_ref[...])
    cum = _chunk_cumsum(dt * a, ch)
    eye = (lax.broadcasted_iota(jnp.int32, (SUBLANES, LANES), 0)
           == lax.broadcasted_iota(jnp.int32, (SUBLANES, LANES), 1)).astype(F32)
    cum_t = lax.dot_general(eye, cum, NT_DIMS, precision=lax.Precision.HIGHEST,
                            preferred_element_type=F32)
    dt_t = lax.dot_general(eye, dt, NT_DIMS, precision=lax.Precision.HIGHEST,
                           preferred_element_type=F32)
    causal = (lax.broadcasted_iota(jnp.int32, (ch, ch), 0)
              >= lax.broadcasted_iota(jnp.int32, (ch, ch), 1))
    dsk = dsk_ref[...]
    ys = []
    for grp in range(SSD_GROUPS):
        bg = bm[:, grp * SSD_DSTATE:(grp + 1) * SSD_DSTATE]
        cg = cm[:, grp * SSD_DSTATE:(grp + 1) * SSD_DSTATE]
        cb = lax.dot_general(cg, bg, NT_DIMS, preferred_element_type=F32)
        for rep in range(SSD_REP):
            hh = grp * SSD_REP + rep
            cum_c = cum[:, hh:hh + 1]
            lmat = jnp.exp(jnp.where(causal, cum_c - cum_t[hh:hh + 1, :], -jnp.inf))
            m = cb * lmat * dt_t[hh:hh + 1, :]
            xh = xs[:, hh * SSD_HEADDIM:(hh + 1) * SSD_HEADDIM]
            y_intra = jnp.dot(m.astype(BF16), xh.astype(BF16), preferred_element_type=F32)
            cl = cum[ch - 1:ch, hh:hh + 1]
            xw = xh * (jnp.exp(cl - cum_c) * dt[:, hh:hh + 1])
            hs = hst_ref[sq, hh]
            y_inter = lax.dot_general(cg, hs.astype(BF16), NT_DIMS, preferred_element_type=F32)
            hst_ref[sq, hh] = jnp.exp(cl) * hs + lax.dot_general(
                xw.astype(BF16), bg, TN_DIMS, preferred_element_type=F32)
            ys.append(y_intra + y_inter * jnp.exp(cum_c) + dsk[:, hh:hh + 1] * xh)
    y = jnp.concatenate(ys, axis=1) * _silu(z)
    mix_ref[rs, GLA_VW:MIX_W] = _rms(y, nw_ref[...])


def _mixer_kernel(qkvg_ref, gk_ref, zx_ref, dt_ref, s0_ref, h0_ref, c0_ref, gnw_ref,
                  cw_ref, cb_ref, dtb_ref, alog_ref, dsk_ref, snw_ref,
                  mix_ref, st_ref, hst_ref, cst_ref, carry_ref, *, nsq, nsub, ch):
    @pl.when(pl.program_id(1) == 0)
    def _():
        st_ref[...] = s0_ref[...]
        hst_ref[...] = h0_ref[...]
        carry_ref[...] = jnp.zeros_like(carry_ref)
        carry_ref[:, SUBLANES - (SSD_CONV - 1):SUBLANES, :] = c0_ref[...]

    _gla_part(qkvg_ref, gk_ref, gnw_ref, mix_ref, st_ref, nsq=nsq, nsub=nsub, ch=ch)
    for sq in range(nsq):
        _ssd_part(zx_ref, dt_ref, cw_ref, cb_ref, dtb_ref, alog_ref, dsk_ref, snw_ref,
                  mix_ref, hst_ref, cst_ref, carry_ref, sq=sq, ch=nsub * ch)


def _mixers(qkvg, gk, zx, dtr, s0t, h0, c0, gnw, cw, cb, dtb, alog, dsk, snw,
            *, row0, nseq, nsq, nblk, nsub, ch):
    assert nseq % nsq == 0 and (nsq == 1 or nblk == 1)
    rb = nsq * nsub * ch
    blk0 = row0 // rb
    row = lambda b, j: (blk0 + b * nblk + j, 0)
    fixed = lambda b, j: (0, 0)
    seq3 = lambda b, j: (b, 0, 0)
    seq4 = lambda b, j: (b, 0, 0, 0)
    return pl.pallas_call(
        functools.partial(_mixer_kernel, nsq=nsq, nsub=nsub, ch=ch),
        grid=(nseq // nsq, nblk),
        in_specs=[
            pl.BlockSpec((rb, QKVG_W), row),
            pl.BlockSpec((rb, GLA_KW), row),
            pl.BlockSpec((rb, ZX_W), row),
            pl.BlockSpec((rb, MISC_W), row),
            pl.BlockSpec((nsq, GLA_DV, GLA_KW), seq3),
            pl.BlockSpec((nsq, SSD_HEADS, SSD_HEADDIM, SSD_DSTATE), seq4),
            pl.BlockSpec((nsq, SSD_CONV - 1, CONV_DIM), seq3),
            pl.BlockSpec((1, GLA_DV), fixed),
            pl.BlockSpec((SSD_CONV, CONV_DIM), fixed),
            pl.BlockSpec((1, CONV_DIM), fixed),
            pl.BlockSpec((1, MISC_W), fixed),
            pl.BlockSpec((1, MISC_W), fixed),
            pl.BlockSpec((1, MISC_W), fixed),
            pl.BlockSpec((1, SSD_INNER), fixed),
        ],
        out_specs=[
            pl.BlockSpec((rb, MIX_W), lambda b, j: (b * nblk + j, 0)),
            pl.BlockSpec((nsq, GLA_DV, GLA_KW), seq3),
            pl.BlockSpec((nsq, SSD_HEADS, SSD_HEADDIM, SSD_DSTATE), seq4),
            pl.BlockSpec((nsq, SSD_CONV - 1, CONV_DIM), seq3),
        ],
        out_shape=[
            jax.ShapeDtypeStruct((nseq * nblk * nsub * ch, MIX_W), F32),
            jax.ShapeDtypeStruct((nseq, GLA_DV, GLA_KW), F32),
            jax.ShapeDtypeStruct((nseq, SSD_HEADS, SSD_HEADDIM, SSD_DSTATE), F32),
            jax.ShapeDtypeStruct((nseq, SSD_CONV - 1, CONV_DIM), F32),
        ],
        scratch_shapes=[pltpu.VMEM((nsq, SUBLANES, CONV_DIM), F32)],
        compiler_params=pltpu.CompilerParams(
            dimension_semantics=("arbitrary", "arbitrary"), vmem_limit_bytes=VMEM_LIMIT),
        name="token_mixers",
    )(qkvg, gk, zx, dtr, s0t, h0, c0, gnw, cw, cb, dtb, alog, dsk, snw)


def _out_proj_kernel(h_ref, mix_ref, w_ref, o_ref):
    o_ref[...] = h_ref[...] + jnp.dot(mix_ref[...].astype(BF16), w_ref[...],
                                      preferred_element_type=F32)


def _out_proj(h, mix, w, tm):
    t_pad = h.shape[0]
    row = lambda i: (i, 0)
    return pl.pallas_call(
        _out_proj_kernel,
        grid=(t_pad // tm,),
        in_specs=[
            pl.BlockSpec((tm, D_MODEL), row),
            pl.BlockSpec((tm, MIX_W), row),
            pl.BlockSpec((MIX_W, D_MODEL), lambda i: (0, 0)),
        ],
        out_specs=pl.BlockSpec((tm, D_MODEL), row),
        out_shape=jax.ShapeDtypeStruct((t_pad, D_MODEL), F32),
        compiler_params=pltpu.CompilerParams(
            dimension_semantics=("arbitrary",), vmem_limit_bytes=VMEM_LIMIT),
        name="out_proj",
    )(h, mix, w)


N_EXTRACT = PEER_TOPK + 1
VAL_ROWS = 24
NOT_RANKED = 255.0


def _top_values(s, tm, want_rank=False):
    rowi = lax.broadcasted_iota(jnp.int32, (VAL_ROWS, tm), 0)
    vals = jnp.full((VAL_ROWS, tm), -jnp.inf, F32)
    rank = jnp.full(s.shape, NOT_RANKED, F32) if want_rank else None
    for i in range(N_EXTRACT):
        m = jnp.max(s, axis=0, keepdims=True)
        vals = jnp.where(rowi == i, m, vals)
        hit = s == m
        if want_rank:
            rank = jnp.where(hit, float(i), rank)
        s = jnp.where(hit, -jnp.inf, s)
    return (vals, rank) if want_rank else vals


def _select_chunk(s1, s2):
    row8 = lax.broadcasted_iota(jnp.int32, (SUBLANES, LANES), 0)
    row24 = lax.broadcasted_iota(jnp.int32, (VAL_ROWS, LANES), 0)
    v1 = _top_values(s1, LANES)
    v2, rank2 = _top_values(s2, LANES, want_rank=True)
    pieces = [jnp.where(row24 < N_EXTRACT, v1[0:1] + v2, -jnp.inf)]
    for i in range(1, SUBLANES):
        lim = N_EXTRACT // (i + 1)
        pieces.append(jnp.where(row8 < lim, v1[i:i + 1] + v2[0:SUBLANES], -jnp.inf))
    pieces.append(v1[SUBLANES:VAL_ROWS] + v2[0:1])
    cand = jnp.concatenate(pieces, axis=0)
    best = _top_values(cand, LANES)
    thr = 0.5 * (best[PEER_TOPK - 1:PEER_TOPK] + best[PEER_TOPK:PEER_TOPK + 1])
    zsum = jnp.sum(jnp.where(cand >= thr, jnp.exp(cand - best[0:1]), 0.0), axis=0, keepdims=True)
    c = thr - s1
    n = jnp.zeros_like(c)
    for j in range(PEER_TOPK):
        n = n + jnp.where(v2[j:j + 1] >= c, 1.0, 0.0)
    return rank2, jnp.exp(s2 - v2[0:1]) / zsum, n, jnp.exp(s1 - v1[0:1])


def _route_kernel(h_ref, nw_ref, wqt_ref, kkt_ref, xnt_ref, r2_ref, p2_ref, n_ref, p1_ref, st_ref,
                  *, tm):
    xn = _rms(h_ref[...], nw_ref[...])
    xnb = xn.astype(BF16)
    xnt_ref[...] = xn.T.astype(BF16)
    qt = lax.dot_general(wqt_ref[...], xnb, NT_DIMS, preferred_element_type=F32)
    st_ref[...] = jnp.dot(kkt_ref[...], qt.astype(BF16), preferred_element_type=F32)
    def per_head(h, carry):
        r1 = pl.multiple_of(h * (2 * PEER_NKEYS), 2 * PEER_NKEYS)
        r0 = pl.multiple_of(h * PEER_NKEYS, PEER_NKEYS)
        for tc in range(tm // LANES):
            cols = slice(tc * LANES, (tc + 1) * LANES)
            s1 = st_ref[pl.ds(r1, PEER_NKEYS), cols]
            s2 = st_ref[pl.ds(r1 + PEER_NKEYS, PEER_NKEYS), cols]
            rank2, p2, n, p1 = _select_chunk(s1, s2)
            r2_ref[pl.ds(r0, PEER_NKEYS), cols] = rank2.astype(BF16)
            p2_ref[pl.ds(r0, PEER_NKEYS), cols] = p2.astype(BF16)
            n_ref[pl.ds(r0, PEER_NKEYS), cols] = n
            p1_ref[pl.ds(r0, PEER_NKEYS), cols] = p1
        return carry

    lax.fori_loop(0, PEER_HEADS, per_head, 0, unroll=2)


def _route(h, nw, wqt, kkt, tm):
    t_pad = h.shape[0]
    hk = PEER_HEADS * PEER_NKEYS
    col = lambda i: (0, i)
    fixed = lambda i: (0, 0)
    return pl.pallas_call(
        functools.partial(_route_kernel, tm=tm),
        grid=(t_pad // tm,),
        in_specs=[
            pl.BlockSpec((tm, D_MODEL), lambda i: (i, 0)),
            pl.BlockSpec((1, D_MODEL), fixed),
            pl.BlockSpec((PEER_HEADS * PEER_QDIM, D_MODEL), fixed),
            pl.BlockSpec((2 * hk, PEER_HEADS * PEER_QDIM), fixed),
        ],
        out_specs=[pl.BlockSpec((D_MODEL, tm), col)] + [pl.BlockSpec((hk, tm), col)] * 4,
        out_shape=[jax.ShapeDtypeStruct((D_MODEL, t_pad), BF16)]
        + [jax.ShapeDtypeStruct((hk, t_pad), dt) for dt in (BF16, BF16, F32, F32)],
        scratch_shapes=[pltpu.VMEM((2 * hk, tm), F32)],
        compiler_params=pltpu.CompilerParams(
            dimension_semantics=("arbitrary",), vmem_limit_bytes=VMEM_LIMIT),
        name="peer_route",
    )(h, nw, wqt, kkt)


def _peer_kernel(xnt_ref, u_ref, vt_ref, r2_in_ref, p2_in_ref, n_ref, p1_ref, h_ref, o_ref,
                 acc_ref, hid_ref, w_ref, r2_ref, p2_ref, *, na, tm):
    e = pl.program_id(1)

    @pl.when(e == 0)
    def _():
        acc_ref[...] = jnp.zeros_like(acc_ref)
        r2_ref[...] = r2_in_ref[...]
        p2_ref[...] = p2_in_ref[...]

    hid_ref[...] = jnp.dot(u_ref[...], xnt_ref[...], preferred_element_type=F32).astype(BF16)
    _weighted_activations(hid_ref, w_ref, r2_ref, p2_ref, n_ref, p1_ref, e, na=na, tm=tm)
    acc_ref[...] += jnp.dot(vt_ref[...], w_ref[...], preferred_element_type=F32)

    @pl.when(e == pl.num_programs(1) - 1)
    def _():
        o_ref[...] = h_ref[...] + acc_ref[...].T


def _weighted_activations(hid_ref, w_ref, r2_ref, p2_ref, n_ref, p1_ref, e, *, na, tm):
    assert na % SUBLANES == 0 and SUBLANES % GATE_GROUP == 0
    a0 = pl.multiple_of(e * na, SUBLANES)
    for tc in range(tm // LANES):
        cols = slice(tc * LANES, (tc + 1) * LANES)
        for al0 in range(0, na, GATE_GROUP):
            gates = [None] * GATE_GROUP
            blk0 = al0 // SUBLANES * SUBLANES
            for h in range(PEER_HEADS):
                rows = slice(h * PEER_NKEYS, (h + 1) * PEER_NKEYS)
                r2 = r2_ref[rows, cols]
                p2 = p2_ref[rows, cols]
                n_blk = n_ref[pl.ds(h * PEER_NKEYS + a0 + blk0, SUBLANES), cols]
                p_blk = p1_ref[pl.ds(h * PEER_NKEYS + a0 + blk0, SUBLANES), cols]
                for k in range(GATE_GROUP):
                    al = al0 - blk0 + k
                    n_row = n_blk[al:al + 1].astype(BF16)
                    p_row = p_blk[al:al + 1].astype(BF16)
                    term = jnp.where(r2 < n_row, p2, 0) * p_row
                    gates[k] = term if gates[k] is None else gates[k] + term
            for k in range(GATE_GROUP):
                erows = slice((al0 + k) * PEER_NKEYS, (al0 + k + 1) * PEER_NKEYS)
                w_ref[erows, cols] = _gelu_tanh(hid_ref[erows, cols]) * gates[k]


def _peer(xnt, u, vt, r2, p2, n, p1, h, tm, na):
    t_pad = h.shape[0]
    hk = PEER_HEADS * PEER_NKEYS
    te = na * PEER_NKEYS
    tok = lambda i, e: (0, i)
    return pl.pallas_call(
        functools.partial(_peer_kernel, na=na, tm=tm),
        grid=(t_pad // tm, PEER_NKEYS // na),
        in_specs=[
            pl.BlockSpec((D_MODEL, tm), tok),
            pl.BlockSpec((te, D_MODEL), lambda i, e: (e, 0)),
            pl.BlockSpec((D_MODEL, te), lambda i, e: (0, e)),
            pl.BlockSpec((hk, tm), tok),
            pl.BlockSpec((hk, tm), tok),
            pl.BlockSpec((hk, tm), tok),
            pl.BlockSpec((hk, tm), tok),
            pl.BlockSpec((tm, D_MODEL), lambda i, e: (i, 0)),
        ],
        out_specs=pl.BlockSpec((tm, D_MODEL), lambda i, e: (i, 0)),
        out_shape=jax.ShapeDtypeStruct((t_pad, D_MODEL), F32),
        scratch_shapes=[pltpu.VMEM((D_MODEL, tm), F32), pltpu.VMEM((te, tm), BF16),
                        pltpu.VMEM((te, tm), BF16), pltpu.VMEM((hk, tm), BF16),
                        pltpu.VMEM((hk, tm), BF16)],
        compiler_params=pltpu.CompilerParams(
            dimension_semantics=("arbitrary", "arbitrary"), vmem_limit_bytes=VMEM_LIMIT),
        name="peer_experts",
    )(xnt, u, vt, r2, p2, n, p1, h)


def _final_norm_kernel(h_ref, w_ref, o_ref):
    o_ref[...] = _rms(h_ref[...], w_ref[...])


def _final_norm(h, w, tm):
    t_pad = h.shape[0]
    return pl.pallas_call(
        _final_norm_kernel,
        grid=(t_pad // tm,),
        in_specs=[pl.BlockSpec((tm, D_MODEL), lambda i: (i, 0)),
                  pl.BlockSpec((1, D_MODEL), lambda i: (0, 0))],
        out_specs=pl.BlockSpec((tm, D_MODEL), lambda i: (i, 0)),
        out_shape=jax.ShapeDtypeStruct((t_pad, D_MODEL), F32),
        name="final_norm",
    )(h, w)


def _pad_lanes(x, width):
    return jnp.pad(x, ((0, 0), (0, width - x.shape[-1])))


def kernel(x_prompt, x_sample, state_gla, state_ssm, state_conv, meta_tokens, norm1_w, w_in, w_gk2, b_gk2, gla_norm_w, conv_w, conv_b, dt_bias, a_log, d_skip, ssd_norm_w, w_out, norm2_w, peer_wq, peer_k1, peer_k2, peer_u, peer_v, final_norm_w):
    bp, seq_p, _ = x_prompt.shape
    bs, seq_s, _ = x_sample.shape
    depth = w_in.shape[0]
    lp = seq_p + N_META
    tp, ts = bp * lp, bs * seq_s
    tm = TOKEN_TILE
    t_pad = -(-(tp + ts) // tm) * tm
    assert lp % PROMPT_ROWS == 0 and seq_s % SUBLANES == 0
    assert bs % SAMPLE_SEQS_PER_STEP == 0 and tp % (SAMPLE_SEQS_PER_STEP * seq_s) == 0
    assert SSD_CONV - 1 <= seq_s and t_pad % ROUTE_TILE == 0

    hp = jnp.concatenate([jnp.broadcast_to(meta_tokens[None], (bp, N_META, D_MODEL)), x_prompt], axis=1)
    h = jnp.concatenate([hp.reshape(tp, D_MODEL), x_sample.reshape(ts, D_MODEL),
                         jnp.zeros((t_pad - tp - ts, D_MODEL), F32)], axis=0)

    offs = [0, GLA_KW, 2 * GLA_KW, 2 * GLA_KW + GLA_VW, QKVG_W, QKVG_W + GLA_RANK,
            QKVG_W + GLA_RANK + SSD_INNER, QKVG_W + GLA_RANK + SSD_INNER + CONV_DIM]
    w_flr = w_in[:, :, offs[4]:offs[5]]
    w_zx = w_in[:, :, offs[5]:offs[7]]
    w_dt = w_in[:, :, offs[7]:offs[7] + SSD_HEADS]
    w_misc = jnp.concatenate(
        [w_dt, w_flr, jnp.zeros((depth, D_MODEL, MISC_W - SSD_HEADS - GLA_RANK), F32)], axis=-1)
    w_proj = jnp.concatenate([w_in[:, :, 0:QKVG_W], w_zx, w_misc], axis=-1).astype(BF16)
    w_gk = jnp.concatenate(
        [jnp.zeros((depth, SSD_HEADS, GLA_KW), F32), w_gk2,
         jnp.zeros((depth, MISC_W - SSD_HEADS - GLA_RANK, GLA_KW), F32)], axis=1).astype(BF16)
    w_out_b = w_out.astype(BF16)
    wq_t = jnp.swapaxes(peer_wq, 1, 2).astype(BF16)
    half = PEER_QDIM // 2
    zeros_k = jnp.zeros((depth, PEER_NKEYS, half), F32)
    pair = jnp.concatenate([jnp.concatenate([peer_k1, zeros_k], axis=2),
                            jnp.concatenate([zeros_k, peer_k2], axis=2)], axis=1)
    kk_t = jnp.einsum('hg,lab->lhagb', jnp.eye(PEER_HEADS, dtype=F32), pair).reshape(
        depth, PEER_HEADS * 2 * PEER_NKEYS, PEER_HEADS * PEER_QDIM).astype(BF16)
    u_b = peer_u.astype(BF16)
    v_t = jnp.swapaxes(peer_v, 1, 2).astype(BF16)
    dtb = _pad_lanes(dt_bias, MISC_W)
    alog = _pad_lanes(a_log, MISC_W)
    dsk = _pad_lanes(d_skip, MISC_W)

    zero_gla = jnp.zeros((bp, GLA_DV, GLA_KW), F32)
    zero_ssm = jnp.zeros((bp, SSD_HEADS, SSD_HEADDIM, SSD_DSTATE), F32)
    zero_conv = jnp.zeros((bp, SSD_CONV - 1, CONV_DIM), F32)
    pad_rows = t_pad - tp - ts

    def gla_state_in(s):
        return jnp.transpose(s, (0, 3, 1, 2)).reshape(s.shape[0], GLA_DV, GLA_KW)

    def gla_state_out(s):
        return jnp.transpose(s.reshape(s.shape[0], GLA_DV, GLA_HEADS, GLA_DK), (0, 2, 3, 1))

    outs = [[] for _ in range(6)]
    for l in range(depth):
        qkvg, gk, zx, dtr = _in_proj(h, norm1_w[l][None], w_proj[l], w_gk[l], b_gk2[l][None], tm)
        mix_w = (gla_norm_w[l][None], conv_w[l], conv_b[l][None], dtb[l][None], alog[l][None],
                 dsk[l][None], ssd_norm_w[l][None])
        mix_p, sg_p, ss_p, sc_p = _mixers(
            qkvg, gk, zx, dtr, zero_gla, zero_ssm, zero_conv, *mix_w, row0=0, nseq=bp, nsq=1,
            nblk=lp // PROMPT_ROWS, nsub=PROMPT_ROWS // GLA_CHUNK, ch=GLA_CHUNK)
        mix_s, sg_s, ss_s, sc_s = _mixers(
            qkvg, gk, zx, dtr, gla_state_in(state_gla[l]), state_ssm[l], state_conv[l], *mix_w,
            row0=tp, nseq=bs, nsq=SAMPLE_SEQS_PER_STEP, nblk=1, nsub=1, ch=seq_s)
        mix = jnp.concatenate([mix_p, mix_s, jnp.zeros((pad_rows, MIX_W), F32)], axis=0)
        h = _out_proj(h, mix, w_out_b[l], tm)
        xnt, r2, p2, n, p1 = _route(h, norm2_w[l][None], wq_t[l], kk_t[l], ROUTE_TILE)
        h = _peer(xnt, u_b[l], v_t[l], r2, p2, n, p1, h, tm, PEER_A_PER_STEP)
        for lst, val in zip(outs, (gla_state_out(sg_p), ss_p, sc_p, gla_state_out(sg_s), ss_s, sc_s)):
            lst.append(val)

    y = _final_norm(h, final_norm_w[None], tm)
    y_prompt = y[:tp].reshape(bp, lp, D_MODEL)[:, N_META:]
    y_sample = y[tp:tp + ts].reshape(bs, seq_s, D_MODEL)
    return (y_prompt, y_sample) + tuple(jnp.stack(o) for o in outs)
```
